```python
import math
import jax, jax.numpy as jnp
from jax import lax
import numpy as np

D_MODEL = 2048
BATCH = 4
SEQ = 2048
DEPTH = 1
DEC_BATCH = 32
DEC_SEQ = 4
PAST_LEN = 8192
PAGE_SIZE = 128

HEAD_DIM = 128
H_SB = 8
H_DSA = 8
W_SB = H_SB * HEAD_DIM
W_DSA = H_DSA * HEAD_DIM
H_IDX = 16
D_IDX = 128
TOPK_MAX = 256
Q_BLOCK = 128
RMS_EPS = 1e-6
NEG_INF = -1e30

IN_SIZES = (W_SB, W_SB, W_SB, W_SB,
            W_DSA, W_DSA, W_DSA, W_DSA,
            H_IDX * D_IDX, D_IDX, H_IDX,
            D_MODEL, D_MODEL)
IN_SPLITS = tuple(sum(IN_SIZES[:i + 1]) for i in range(len(IN_SIZES) - 1))
N_IN = sum(IN_SIZES)

kernel_name = "stickbreak_dsa_hybrid_step"


def rmsnorm(x, g):
    xf = x.astype(jnp.float32)
    y = xf * lax.rsqrt(jnp.mean(xf * xf, axis=-1, keepdims=True) + RMS_EPS)
    return (y * g.astype(jnp.float32)).astype(x.dtype)


def alibi_slopes():
    return jnp.exp2(-8.0 * (jnp.arange(H_DSA, dtype=jnp.float32) + 1.0) / H_DSA)


take_rows = jax.vmap(lambda rows, ix: rows[ix])


def branch_inputs(x, c, norm_g, w_ada, b_ada, w_in):
    B, T, _ = x.shape
    mod = jax.nn.silu(c) @ w_ada + b_ada
    shift, scale, gate = jnp.split(mod[:, None, :], 3, axis=-1)
    xn = rmsnorm(x, norm_g) * (1 + scale) + shift
    (sb_q, sb_k, sb_v, sb_z, ds_q, ds_k, ds_v, ds_z,
     ix_q, ix_k, ix_w, g_sb, g_ds) = jnp.split(xn @ w_in, IN_SPLITS, axis=-1)
    hd = lambda t, n: t.reshape(B, T, n, -1)
    return dict(gate=gate,
                sb_q=hd(sb_q, H_SB), sb_k=hd(sb_k, H_SB), sb_v=hd(sb_v, H_SB), sb_z=sb_z,
                ds_q=hd(ds_q, H_DSA), ds_k=hd(ds_k, H_DSA), ds_v=hd(ds_v, H_DSA), ds_z=ds_z,
                ix_q=hd(ix_q, H_IDX), ix_k=ix_k, ix_w=ix_w, g_sb=g_sb, g_ds=g_ds)


def branch_outputs(x, p, sb_o, ds_o, w_up_sb, w_up_dsa, w_out):
    B, T, _ = x.shape
    h_sb = (sb_o.reshape(B, T, W_SB) * jax.nn.silu(p['sb_z'])) @ w_up_sb
    h_ds = (ds_o.reshape(B, T, W_DSA) * jax.nn.silu(p['ds_z'])) @ w_up_dsa
    merged = jax.nn.sigmoid(p['g_sb']) * h_sb + jax.nn.sigmoid(p['g_ds']) * h_ds
    return x + p['gate'] * (merged @ w_out)


def stick_breaking(q, k, v, qpos, kpos):
    z = jnp.einsum('bqhd,bkhd->bhqk', q, k).astype(jnp.float32) * (HEAD_DIM ** -0.5)
    mask = kpos[None, :] < qpos[:, None]
    log_1m = jnp.where(mask, jax.nn.log_sigmoid(-z), 0.0)
    between = lax.cumsum(log_1m, axis=log_1m.ndim - 1, reverse=True) - log_1m
    a = jnp.where(mask, jnp.exp(jax.nn.log_sigmoid(z) + between), 0.0)
    return jnp.einsum('bhqk,bkhd->bqhd', a.astype(v.dtype), v)


def indexer_scores(iq, iw, ik, qpos, kpos):
    s = jnp.einsum('bqhd,bkd->bqhk', iq, ik).astype(jnp.float32) * (D_IDX ** -0.5)
    score = jnp.einsum('bqh,bqhk->bqk', iw.astype(jnp.float32) * (H_IDX ** -0.5), jax.nn.relu(s))
    return jnp.where(kpos[None, None, :] <= qpos[None, :, None], score, NEG_INF)


def dsa_attend(q, kg, vg, qpos, idx):
    valid = idx <= qpos[None, :, None]
    dist = (qpos[None, :, None] - idx).astype(jnp.float32)
    s = jnp.einsum('bqhd,bqkhd->bhqk', q, kg).astype(jnp.float32) * (HEAD_DIM ** -0.5)
    s = s - alibi_slopes()[None, :, None, None] * dist[:, None]
    s = jnp.where(valid[:, None], s, NEG_INF)
    p = jax.nn.softmax(s, axis=-1)
    return jnp.einsum('bhqk,bqkhd->bqhd', p.astype(vg.dtype), vg)


def sb_prompt(q, k, v):
    B, T, H, hd = q.shape
    nb = T // Q_BLOCK
    kpos = jnp.arange(T)
    qb = q.reshape(B, nb, Q_BLOCK, H, hd).swapaxes(0, 1)

    def blk(args):
        qi, i = args
        return stick_breaking(qi, k, v, i * Q_BLOCK + jnp.arange(Q_BLOCK), kpos)

    out = lax.map(blk, (qb, jnp.arange(nb)))
    return out.swapaxes(0, 1).reshape(B, T, H, hd)


def dsa_prompt(q, k, v, iq, iw, ik):
    B, T, H, hd = q.shape
    nb = T // Q_BLOCK
    topk = min(TOPK_MAX, T // 4)
    kpos = jnp.arange(T)
    to_blocks = lambda t: t.reshape((B, nb, Q_BLOCK) + t.shape[2:]).swapaxes(0, 1)

    def blk(args):
        qi, iqi, iwi, i = args
        qpos = i * Q_BLOCK + jnp.arange(Q_BLOCK)
        _, idx = lax.top_k(indexer_scores(iqi, iwi, ik, qpos, kpos), topk)
        return dsa_attend(qi, take_rows(k, idx), take_rows(v, idx), qpos, idx)

    out = lax.map(blk, (to_blocks(q), to_blocks(iq), to_blocks(iw), jnp.arange(nb)))
    return out.swapaxes(0, 1).reshape(B, T, H, hd)


def gather_pages(pool, page_table):
    db, n_pages = page_table.shape
    return pool[page_table].reshape((db, n_pages * PAGE_SIZE) + pool.shape[2:])


def gather_paged_rows(pool, page_table, new_rows, idx):
    db = idx.shape[0]
    past_len = page_table.shape[1] * PAGE_SIZE
    ip = jnp.minimum(idx, past_len - 1)
    phys = jnp.take_along_axis(page_table, (ip // PAGE_SIZE).reshape(db, -1), axis=1).reshape(idx.shape)
    past = pool[phys, ip % PAGE_SIZE]
    new = take_rows(new_rows, jnp.clip(idx - past_len, 0, new_rows.shape[1] - 1))
    in_past = (idx < past_len).reshape(idx.shape + (1,) * (past.ndim - idx.ndim))
    return jnp.where(in_past, past, new)


def setup_inputs(seed: int = 0) -> dict:
    key = jax.random.key(seed)
    ks = jax.random.split(key, 20)
    n_pages = PAST_LEN // PAGE_SIZE
    n_used = DEC_BATCH * n_pages
    n_pool = n_used + n_used // 4
    perm = jax.random.permutation(ks[0], n_pool)
    page_table = perm[:n_used].reshape(DEC_BATCH, n_pages).astype(jnp.int32)
    nrm = lambda k, shape, s=1.0: jax.random.normal(k, shape, jnp.float32) * s
    return {
        "x_prompt": nrm(ks[1], (BATCH, SEQ, D_MODEL)),
        "x_sample": nrm(ks[2], (DEC_BATCH, DEC_SEQ, D_MODEL)),
        "c_prompt": nrm(ks[3], (BATCH, D_MODEL)),
        "c_sample": nrm(ks[4], (DEC_BATCH, D_MODEL)),
        "cache_sb_k": nrm(ks[5], (DEPTH, n_pool, PAGE_SIZE, H_SB, HEAD_DIM)),
        "cache_sb_v": nrm(ks[6], (DEPTH, n_pool, PAGE_SIZE, H_SB, HEAD_DIM)),
        "cache_dsa_k": nrm(ks[7], (DEPTH, n_pool, PAGE_SIZE, H_DSA, HEAD_DIM)),
        "cache_dsa_v": nrm(ks[8], (DEPTH, n_pool, PAGE_SIZE, H_DSA, HEAD_DIM)),
        "cache_idx_k": nrm(ks[9], (DEPTH, n_pool, PAGE_SIZE, D_IDX)),
        "page_table": page_table,
        "norm_g": 1.0 + nrm(ks[10], (DEPTH, D_MODEL), 0.01),
        "w_ada": nrm(ks[11], (DEPTH, D_MODEL, 3 * D_MODEL), 0.5 * D_MODEL ** -0.5),
        "b_ada": nrm(ks[12], (DEPTH, 3 * D_MODEL), 0.01),
        "w_in": nrm(ks[13], (DEPTH, D_MODEL, N_IN), D_MODEL ** -0.5),
        "w_up_sb": nrm(ks[14], (DEPTH, W_SB, D_MODEL), W_SB ** -0.5),
        "w_up_dsa": nrm(ks[15], (DEPTH, W_DSA, D_MODEL), W_DSA ** -0.5),
        "w_out": nrm(ks[16], (DEPTH, D_MODEL, D_MODEL), D_MODEL ** -0.5),
        "final_g": 1.0 + nrm(ks[17], (D_MODEL,), 0.01),
    }


def reference(x_prompt, x_sample, c_prompt, c_sample, cache_sb_k, cache_sb_v, cache_dsa_k,
              cache_dsa_v, cache_idx_k, page_table, norm_g, w_ada, b_ada, w_in, w_up_sb,
              w_up_dsa, w_out, final_g):
    xp, xs = x_prompt, x_sample
    past_len = page_table.shape[1] * PAGE_SIZE
    n_new = x_sample.shape[1]
    qpos_s = past_len + jnp.arange(n_new)
    kpos_s = jnp.arange(past_len + n_new)
    topk_s = min(TOPK_MAX, (past_len + n_new) // 4)
    rows = [[] for _ in range(10)]
    for l in range(DEPTH):
        p = branch_inputs(xp, c_prompt, norm_g[l], w_ada[l], b_ada[l], w_in[l])
        sb_o = sb_prompt(p['sb_q'], p['sb_k'], p['sb_v'])
        ds_o = dsa_prompt(p['ds_q'], p['ds_k'], p['ds_v'], p['ix_q'], p['ix_w'], p['ix_k'])
        xp = branch_outputs(xp, p, sb_o, ds_o, w_up_sb[l], w_up_dsa[l], w_out[l])

        s = branch_inputs(xs, c_sample, norm_g[l], w_ada[l], b_ada[l], w_in[l])
        k_all = jnp.concatenate([gather_pages(cache_sb_k[l], page_table), s['sb_k']], axis=1)
        v_all = jnp.concatenate([gather_pages(cache_sb_v[l], page_table), s['sb_v']], axis=1)
        sb_os = stick_breaking(s['sb_q'], k_all, v_all, qpos_s, kpos_s)
        ik_all = jnp.concatenate([gather_pages(cache_idx_k[l], page_table), s['ix_k']], axis=1)
        _, idx = lax.top_k(indexer_scores(s['ix_q'], s['ix_w'], ik_all, qpos_s, kpos_s), topk_s)
        kg = gather_paged_rows(cache_dsa_k[l], page_table, s['ds_k'], idx)
        vg = gather_paged_rows(cache_dsa_v[l], page_table, s['ds_v'], idx)
        ds_os = dsa_attend(s['ds_q'], kg, vg, qpos_s, idx)
        xs = branch_outputs(xs, s, sb_os, ds_os, w_up_sb[l], w_up_dsa[l], w_out[l])

        for lst, r in zip(rows, (p['sb_k'], p['sb_v'], p['ds_k'], p['ds_v'], p['ix_k'],
                                 s['sb_k'], s['sb_v'], s['ds_k'], s['ds_v'], s['ix_k'])):
            lst.append(r)
    y_prompt = rmsnorm(xp, final_g)
    y_sample = rmsnorm(xs, final_g)
    (sb_k_p, sb_v_p, ds_k_p, ds_v_p, ix_k_p,
     sb_k_s, sb_v_s, ds_k_s, ds_v_s, ix_k_s) = [jnp.stack(r, axis=0) for r in rows]
    return (y_prompt, y_sample, sb_k_p, sb_v_p, ds_k_p, ds_v_p, ix_k_p, sb_k_s, sb_v_s, ds_k_s, ds_v_s, ix_k_s)
```

```python
import functools

import jax
import jax.numpy as jnp
from jax import lax
from jax.experimental import pallas as pl
from jax.experimental.pallas import tpu as pltpu

F32 = jnp.float32
BF16 = jnp.bfloat16

HEAD_DIM = 128
H_SB = 8
H_DSA = 8
H_IDX = 16
D_IDX = 128
PAGE_SIZE = 128
TOPK_MAX = 256
RMS_EPS = 1e-6
NEG_INF = -1e30
BISECT_ITERS = 32
LANES = 128
VMEM_MB = 56


def _params(sem, vmem_mb=VMEM_MB):
    return pltpu.CompilerParams(dimension_semantics=sem, vmem_limit_bytes=vmem_mb << 20)


def _div_pow2(x, n):
    assert n & (n - 1) == 0
    return x >> (n.bit_length() - 1)


def _mod_pow2(x, n):
    assert n & (n - 1) == 0
    return x & (n - 1)


def _dot(a, b):
    return jnp.dot(a, b, preferred_element_type=F32)


def _dot_nt(a, b):
    return lax.dot_general(a, b, (((1,), (1,)), ((), ())), preferred_element_type=F32)


def _mod_kernel(c_ref, w_ref, b_ref, o_ref):
    c = c_ref[...]
    a = (c * jax.nn.sigmoid(c)).astype(BF16)
    o_ref[...] = _dot(a, w_ref[...].astype(BF16)) + b_ref[...]


def _ada_mod(c, w_ada, b_ada):
    r, d = c.shape
    n = w_ada.shape[1]
    tn = 512
    return pl.pallas_call(
        _mod_kernel,
        grid=(n // tn,),
        in_specs=[pl.BlockSpec((r, d), lambda j: (0, 0)),
                  pl.BlockSpec((d, tn), lambda j: (0, j)),
                  pl.BlockSpec((1, tn), lambda j: (0, j))],
        out_specs=pl.BlockSpec((r, tn), lambda j: (0, j)),
        out_shape=jax.ShapeDtypeStruct((r, n), F32),
        compiler_params=_params(("arbitrary",)),
        name="ada_mod",
    )(c, w_ada, b_ada.reshape(1, n))


def _xn_kernel(x_ref, g_ref, sc_ref, sh_ref, o_ref):
    x = x_ref[0]
    y = x * lax.rsqrt(jnp.mean(x * x, axis=-1, keepdims=True) + RMS_EPS) * g_ref[...]
    o_ref[0] = (y * (1.0 + sc_ref[0]) + sh_ref[0]).astype(o_ref.dtype)


def _xn(x3, g, scale3, shift3, tr):
    gn, r, d = x3.shape
    mr = scale3.shape[1]
    mblk = 1 if mr == 1 else tr
    mod_map = (lambda b, i: (b, 0, 0)) if mr == 1 else (lambda b, i: (b, i, 0))
    return pl.pallas_call(
        _xn_kernel,
        grid=(gn, r // tr),
        in_specs=[pl.BlockSpec((1, tr, d), lambda b, i: (b, i, 0)),
                  pl.BlockSpec((1, d), lambda b, i: (0, 0)),
                  pl.BlockSpec((1, mblk, d), mod_map),
                  pl.BlockSpec((1, mblk, d), mod_map)],
        out_specs=pl.BlockSpec((1, tr, d), lambda b, i: (b, i, 0)),
        out_shape=jax.ShapeDtypeStruct((gn, r, d), BF16),
        compiler_params=_params(("arbitrary", "arbitrary")),
        name="xn",
    )(x3, g.reshape(1, d), scale3, shift3)


def _mm_kernel(x_ref, w_ref, o_ref):
    o_ref[...] = _dot(x_ref[...], w_ref[...]).astype(o_ref.dtype)


def _mm(x, w, tm, name):
    m, k = x.shape
    n = w.shape[1]
    tn = min(n, 1024)
    return pl.pallas_call(
        _mm_kernel,
        grid=(m // tm, n // tn),
        in_specs=[pl.BlockSpec((tm, k), lambda i, j: (i, 0)),
                  pl.BlockSpec((k, tn), lambda i, j: (0, j))],
        out_specs=pl.BlockSpec((tm, tn), lambda i, j: (i, j)),
        out_shape=jax.ShapeDtypeStruct((m, n), F32),
        compiler_params=_params(("arbitrary", "arbitrary")),
        name=name,
    )(x, w)


def _suffix_matrix(n):
    j = lax.broadcasted_iota(jnp.int32, (n, n), 0)
    s = lax.broadcasted_iota(jnp.int32, (n, n), 1)
    return jnp.where(j > s, 1.0, 0.0).astype(BF16)


def _stick_block(z, mask, run, umat):
    sp = jnp.log(1.0 + jnp.exp(-jnp.abs(z)))
    log_b = jnp.minimum(z, 0.0) - sp
    log_1mb = jnp.minimum(-z, 0.0) - sp
    l = jnp.where(mask, log_1mb, 0.0)
    l_hi = l.astype(BF16)
    l_lo = (l - l_hi.astype(F32)).astype(BF16)
    between = _dot(l_hi, umat) + _dot(l_lo, umat) + run
    a = jnp.where(mask, jnp.exp(log_b + between), 0.0)
    return a, run + jnp.sum(l, axis=-1, keepdims=True)


def _sb_prompt_kernel(q_ref, k_ref, v_ref, o_ref, *, tq, scale):
    i = pl.program_id(2)
    qb = q_ref[0].astype(BF16)
    umat = _suffix_matrix(tq)
    row = lax.broadcasted_iota(jnp.int32, (tq, tq), 0)
    col = lax.broadcasted_iota(jnp.int32, (tq, tq), 1)

    def body(jj, carry):
        acc, run = carry
        j = i - jj
        ks = pl.multiple_of(j * tq, tq)
        kb = k_ref[0, pl.ds(ks, tq), :].astype(BF16)
        vb = v_ref[0, pl.ds(ks, tq), :].astype(BF16)
        z = _dot_nt(qb, kb) * scale
        mask = (col + j * tq) < (row + i * tq)
        a, run = _stick_block(z, mask, run, umat)
        return acc + _dot(a.astype(BF16), vb), run

    acc, _ = lax.fori_loop(0, i + 1, body,
                           (jnp.zeros((tq, HEAD_DIM), F32), jnp.zeros((tq, 1), F32)))
    o_ref[0] = acc


def _sb_prompt(q3, k3, v3, tq=256):
    b, t, w = q3.shape
    nh = w // HEAD_DIM
    kern = functools.partial(_sb_prompt_kernel, tq=tq, scale=HEAD_DIM ** -0.5)
    return pl.pallas_call(
        kern,
        grid=(b, nh, t // tq),
        in_specs=[pl.BlockSpec((1, tq, HEAD_DIM), lambda bi, h, i: (bi, i, h)),
                  pl.BlockSpec((1, t, HEAD_DIM), lambda bi, h, i: (bi, 0, h)),
                  pl.BlockSpec((1, t, HEAD_DIM), lambda bi, h, i: (bi, 0, h))],
        out_specs=pl.BlockSpec((1, tq, HEAD_DIM), lambda bi, h, i: (bi, i, h)),
        out_shape=jax.ShapeDtypeStruct((b, t, w), F32),
        compiler_params=_params(("arbitrary",) * 3),
        name="sb_prompt",
    )(q3, k3, v3)


def _topk_bias(score, valid, k):
    big = 3e38
    lo = jnp.min(jnp.where(valid, score, big), axis=-1, keepdims=True)
    hi = jnp.max(jnp.where(valid, score, -big), axis=-1, keepdims=True)
    msc = jnp.where(valid, score, -big)

    def body(_, carry):
        lo, hi = carry
        mid = 0.5 * (lo + hi)
        cnt = jnp.sum(jnp.where(msc >= mid, 1.0, 0.0), axis=-1, keepdims=True)
        ge = cnt >= k
        return jnp.where(ge, mid, lo), jnp.where(ge, hi, mid)

    lo, _ = lax.fori_loop(0, BISECT_ITERS, body, (lo, hi))
    return jnp.where(msc >= lo, 0.0, NEG_INF)


def _idx_prompt_kernel(iq_ref, iw_ref, ik_ref, o_ref, *, tq, topk, wscale):
    i = pl.program_id(1)
    t = ik_ref.shape[1]
    ik = ik_ref[0].astype(BF16)
    w = iw_ref[0] * wscale
    score = jnp.zeros((tq, t), F32)
    for h in range(H_IDX):
        s = _dot_nt(iq_ref[0, :, h * D_IDX:(h + 1) * D_IDX].astype(BF16), ik)
        score = score + w[:, h:h + 1] * jnp.maximum(s, 0.0)
    qpos = lax.broadcasted_iota(jnp.int32, (tq, t), 0) + i * tq
    kpos = lax.broadcasted_iota(jnp.int32, (tq, t), 1)
    o_ref[0] = _topk_bias(score, kpos <= qpos, topk).astype(o_ref.dtype)


def _idx_prompt(iq3, iw3, ik3, topk, tq=256):
    b, t, _ = iq3.shape
    kern = functools.partial(_idx_prompt_kernel, tq=tq, topk=float(topk),
                             wscale=(D_IDX ** -0.5) * (H_IDX ** -0.5))
    return pl.pallas_call(
        kern,
        grid=(b, t // tq),
        in_specs=[pl.BlockSpec((1, tq, H_IDX * D_IDX), lambda bi, i: (bi, i, 0)),
                  pl.BlockSpec((1, tq, LANES), lambda bi, i: (bi, i, 0)),
                  pl.BlockSpec((1, t, D_IDX), lambda bi, i: (bi, 0, 0))],
        out_specs=pl.BlockSpec((1, tq, t), lambda bi, i: (bi, i, 0)),
        out_shape=jax.ShapeDtypeStruct((b, t, t), BF16),
        compiler_params=_params(("arbitrary",) * 2),
        name="idx_prompt",
    )(iq3, iw3, ik3)


def _dsa_prompt_kernel(q_ref, k_ref, v_ref, bias_ref, o_ref, *, tq, scale):
    h = pl.program_id(1)
    i = pl.program_id(2)
    t = k_ref.shape[1]
    s = _dot_nt(q_ref[0].astype(BF16), k_ref[0].astype(BF16)) * scale
    qpos = lax.broadcasted_iota(jnp.int32, (tq, t), 0) + i * tq
    kpos = lax.broadcasted_iota(jnp.int32, (tq, t), 1)
    slope = jnp.exp2(-(jnp.zeros((1, 1), F32) + (h + 1).astype(F32)))
    s = s - slope * (qpos - kpos).astype(F32) + bias_ref[0].astype(F32)
    m = jnp.max(s, axis=-1, keepdims=True)
    p = jnp.exp(s - m)
    l = jnp.sum(p, axis=-1, keepdims=True)
    o_ref[0] = _dot(p.astype(BF16), v_ref[0].astype(BF16)) / l


def _dsa_prompt(q3, k3, v3, bias, tq=256):
    b, t, w = q3.shape
    nh = w // HEAD_DIM
    kern = functools.partial(_dsa_prompt_kernel, tq=tq, scale=HEAD_DIM ** -0.5)
    return pl.pallas_call(
        kern,
        grid=(b, nh, t // tq),
        in_specs=[pl.BlockSpec((1, tq, HEAD_DIM), lambda bi, h, i: (bi, i, h)),
                  pl.BlockSpec((1, t, HEAD_DIM), lambda bi, h, i: (bi, 0, h)),
                  pl.BlockSpec((1, t, HEAD_DIM), lambda bi, h, i: (bi, 0, h)),
                  pl.BlockSpec((1, tq, t), lambda bi, h, i: (bi, i, 0))],
        out_specs=pl.BlockSpec((1, tq, HEAD_DIM), lambda bi, h, i: (bi, i, h)),
        out_shape=jax.ShapeDtypeStruct((b, t, w), F32),
        compiler_params=_params(("arbitrary",) * 3),
        name="dsa_prompt",
    )(q3, k3, v3, bias)


QROWS = 8


def _page_specs(n_per_step, n_pages, tail, reverse):
    specs = []
    for p in range(n_per_step):
        def imap(b, g, pt, p=p):
            j = jnp.minimum(g * n_per_step + p, n_pages - 1)
            if reverse:
                j = n_pages - 1 - j
            return (pt[b * n_pages + j],) + (0,) * (1 + len(tail))
        specs.append(pl.BlockSpec((1, PAGE_SIZE) + tuple(tail), imap))
    return specs


def _paged_heads(ref):
    return lambda h: ref[0, :, h, :]


def _flat_heads(ref):
    return lambda h: ref[:, h * HEAD_DIM:(h + 1) * HEAD_DIM]


def _heads_qk(q, get_k, n_h):
    return jnp.concatenate(
        [_dot_nt(q[h * QROWS:(h + 1) * QROWS].astype(BF16), get_k(h).astype(BF16))
         for h in range(n_h)], axis=0)


def _heads_pv(p, get_v, n_h):
    return jnp.concatenate(
        [_dot(p[h * QROWS:(h + 1) * QROWS].astype(BF16), get_v(h).astype(BF16))
         for h in range(n_h)], axis=0)


def _store_heads(o_ref, acc, n_new, n_h):
    for h in range(n_h):
        o_ref[0, :, h * HEAD_DIM:(h + 1) * HEAD_DIM] = acc[h * QROWS:h * QROWS + n_new, :]


def _fill_new_rows(pad_ref, new_ref, n_new):
    pad_ref[...] = jnp.zeros_like(pad_ref)
    pad_ref[0:n_new, :] = new_ref[0]


def _sb_sample_kernel(pt_ref, q_ref, knew_ref, vnew_ref, *rest, n_pp, n_pages, n_new, scale):
    k_refs = rest[:n_pp]
    v_refs = rest[n_pp:2 * n_pp]
    o_ref, acc_ref, run_ref, kpad_ref, vpad_ref = rest[2 * n_pp:]
    g = pl.program_id(1)
    past_len = n_pages * PAGE_SIZE
    rows = q_ref.shape[1]
    q = q_ref[0]
    umat = _suffix_matrix(PAGE_SIZE)
    col = lax.broadcasted_iota(jnp.int32, (rows, PAGE_SIZE), 1)
    qi = _mod_pow2(lax.broadcasted_iota(jnp.int32, (rows, PAGE_SIZE), 0), QROWS)
    qpos = past_len + jnp.minimum(qi, n_new - 1)

    def visit(get_k, get_v, kpos0, acc, run):
        z = _heads_qk(q, get_k, H_SB) * scale
        a, run = _stick_block(z, (col + kpos0) < qpos, run, umat)
        return acc + _heads_pv(a, get_v, H_SB), run

    @pl.when(g == 0)
    def _():
        _fill_new_rows(kpad_ref, knew_ref, n_new)
        _fill_new_rows(vpad_ref, vnew_ref, n_new)
        acc, run = visit(_flat_heads(kpad_ref), _flat_heads(vpad_ref), past_len,
                         jnp.zeros(acc_ref.shape, F32), jnp.zeros(run_ref.shape, F32))
        acc_ref[...] = acc
        run_ref[...] = run

    acc = acc_ref[...]
    run = run_ref[...]
    for p in range(n_pp):
        page = n_pages - 1 - (g * n_pp + p)
        acc, run = visit(_paged_heads(k_refs[p]), _paged_heads(v_refs[p]), page * PAGE_SIZE, acc, run)
    acc_ref[...] = acc
    run_ref[...] = run

    @pl.when(g == pl.num_programs(1) - 1)
    def _():
        _store_heads(o_ref, acc, n_new, H_SB)


def _sb_sample(page_table, q, knew, vnew, pool_k, pool_v, n_pp=4):
    db, n_pages = page_table.shape
    n_new, w = knew.shape[1], knew.shape[2]
    rows = q.shape[1]
    tail = pool_k.shape[2:]
    kern = functools.partial(_sb_sample_kernel, n_pp=n_pp, n_pages=n_pages, n_new=n_new,
                             scale=HEAD_DIM ** -0.5)
    row_spec = lambda r, c: pl.BlockSpec((1, r, c), lambda b, g, pt: (b, 0, 0))
    grid_spec = pltpu.PrefetchScalarGridSpec(
        num_scalar_prefetch=1,
        grid=(db, n_pages // n_pp),
        in_specs=[row_spec(rows, HEAD_DIM), row_spec(n_new, w), row_spec(n_new, w)]
        + _page_specs(n_pp, n_pages, tail, True) + _page_specs(n_pp, n_pages, tail, True),
        out_specs=row_spec(n_new, w),
        scratch_shapes=[pltpu.VMEM((rows, HEAD_DIM), F32), pltpu.VMEM((rows, 1), F32),
                        pltpu.VMEM((PAGE_SIZE, w), F32), pltpu.VMEM((PAGE_SIZE, w), F32)],
    )
    return pl.pallas_call(
        kern, grid_spec=grid_spec,
        out_shape=jax.ShapeDtypeStruct((db, n_new, w), F32),
        compiler_params=_params(("arbitrary",) * 2),
        name="sb_sample",
    )(page_table.reshape(-1), q, knew, vnew, *([pool_k] * n_pp), *([pool_v] * n_pp))


def _idx_sample_kernel(pt_ref, iq_ref, w_ref, iknew_ref, *rest, n_pp, n_new):
    ik_refs = rest[:n_pp]
    o_ref, pad_ref = rest[n_pp:]
    g = pl.program_id(1)
    iq = iq_ref[0].astype(BF16)
    w = w_ref[0]

    def page_scores(ikpage):
        s = jnp.maximum(_dot_nt(iq, ikpage.astype(BF16)), 0.0) * w
        return jnp.sum(s.reshape(n_new, H_IDX, PAGE_SIZE), axis=1)

    @pl.when(g < pl.num_programs(1) - 1)
    def _():
        for p in range(n_pp):
            o_ref[0, :, p * PAGE_SIZE:(p + 1) * PAGE_SIZE] = page_scores(ik_refs[p][0])

    @pl.when(g == pl.num_programs(1) - 1)
    def _():
        pad_ref[...] = jnp.zeros_like(pad_ref)
        pad_ref[0:n_new, :] = iknew_ref[0]
        o_ref[0] = jnp.zeros(o_ref.shape[1:], F32)
        o_ref[0, :, 0:PAGE_SIZE] = page_scores(pad_ref[...])


def _idx_sample(page_table, iq, wcol, iknew, pool_ik, n_pp=8):
    db, n_pages = page_table.shape
    n_new = iknew.shape[1]
    rows = iq.shape[1]
    n_groups = n_pages // n_pp
    kern = functools.partial(_idx_sample_kernel, n_pp=n_pp, n_new=n_new)
    grid_spec = pltpu.PrefetchScalarGridSpec(
        num_scalar_prefetch=1,
        grid=(db, n_groups + 1),
        in_specs=[pl.BlockSpec((1, rows, D_IDX), lambda b, g, pt: (b, 0, 0)),
                  pl.BlockSpec((1, rows, 1), lambda b, g, pt: (b, 0, 0)),
                  pl.BlockSpec((1, n_new, D_IDX), lambda b, g, pt: (b, 0, 0))]
        + _page_specs(n_pp, n_pages, (D_IDX,), False),
        out_specs=pl.BlockSpec((1, n_new, n_pp * PAGE_SIZE), lambda b, g, pt: (b, 0, g)),
        scratch_shapes=[pltpu.VMEM((PAGE_SIZE, D_IDX), F32)],
    )
    return pl.pallas_call(
        kern, grid_spec=grid_spec,
        out_shape=jax.ShapeDtypeStruct((db, n_new, (n_groups + 1) * n_pp * PAGE_SIZE), F32),
        compiler_params=_params(("arbitrary",) * 2),
        name="idx_sample",
    )(page_table.reshape(-1), iq, wcol, iknew, *([pool_ik] * n_pp))


def _topk_mask_kernel(s_ref, o_ref, *, n_new, past_len, topk):
    shape = s_ref.shape
    qpos = past_len + _mod_pow2(lax.broadcasted_iota(jnp.int32, shape, 0), n_new)
    kpos = lax.broadcasted_iota(jnp.int32, shape, 1)
    o_ref[...] = _topk_bias(s_ref[...], kpos <= qpos, topk)


def _topk_mask(scores2, n_new, past_len, topk):
    kern = functools.partial(_topk_mask_kernel, n_new=n_new, past_len=past_len, topk=float(topk))
    return pl.pallas_call(
        kern,
        grid=(1,),
        in_specs=[pl.BlockSpec(scores2.shape, lambda i: (0, 0))],
        out_specs=pl.BlockSpec(scores2.shape, lambda i: (0, 0)),
        out_shape=jax.ShapeDtypeStruct(scores2.shape, F32),
        compiler_params=_params(("arbitrary",)),
        name="topk_mask",
    )(scores2)


def _dsa_sample_kernel(pt_ref, q_ref, knew_ref, vnew_ref, bias_ref, bias_new_ref, *rest,
                       n_pp, n_pages, n_new, scale):
    k_refs = rest[:n_pp]
    v_refs = rest[n_pp:2 * n_pp]
    o_ref, acc_ref, m_ref, l_ref, kpad_ref, vpad_ref = rest[2 * n_pp:]
    g = pl.program_id(1)
    past_len = n_pages * PAGE_SIZE
    rows = q_ref.shape[1]
    q = q_ref[0]
    rid = lax.broadcasted_iota(jnp.int32, (rows, PAGE_SIZE), 0)
    col = lax.broadcasted_iota(jnp.int32, (rows, PAGE_SIZE), 1)
    qi = _mod_pow2(rid, QROWS)
    slope = jnp.exp2(-(_div_pow2(rid, QROWS) + 1).astype(F32))
    qpos = past_len + jnp.minimum(qi, n_new - 1)

    def visit(get_k, get_v, bias4, kpos0, acc, m, l):
        bias = jnp.zeros((rows, PAGE_SIZE), F32)
        for r in range(n_new):
            bias = jnp.where(qi == r, bias4[r:r + 1, :], bias)
        s = _heads_qk(q, get_k, H_DSA) * scale
        s = s - slope * (qpos - (col + kpos0)).astype(F32) + bias
        m_new = jnp.maximum(m, jnp.max(s, axis=-1, keepdims=True))
        alpha = jnp.exp(m - m_new)
        p = jnp.exp(s - m_new)
        l = alpha * l + jnp.sum(p, axis=-1, keepdims=True)
        acc = alpha * acc + _heads_pv(p, get_v, H_DSA)
        return acc, m_new, l

    @pl.when(g == 0)
    def _():
        _fill_new_rows(kpad_ref, knew_ref, n_new)
        _fill_new_rows(vpad_ref, vnew_ref, n_new)
        acc, m, l = visit(_flat_heads(kpad_ref), _flat_heads(vpad_ref), bias_new_ref[0],
                          past_len, jnp.zeros(acc_ref.shape, F32),
                          jnp.full(m_ref.shape, NEG_INF, F32), jnp.zeros(l_ref.shape, F32))
        acc_ref[...] = acc
        m_ref[...] = m
        l_ref[...] = l

    acc = acc_ref[...]
    m = m_ref[...]
    l = l_ref[...]
    for p in range(n_pp):
        acc, m, l = visit(_paged_heads(k_refs[p]), _paged_heads(v_refs[p]),
                          bias_ref[0, :, p * PAGE_SIZE:(p + 1) * PAGE_SIZE],
                          (g * n_pp + p) * PAGE_SIZE, acc, m, l)
    acc_ref[...] = acc
    m_ref[...] = m
    l_ref[...] = l

    @pl.when(g == pl.num_programs(1) - 1)
    def _():
        _store_heads(o_ref, acc / l, n_new, H_DSA)


def _dsa_sample(page_table, q, knew, vnew, bias3, pool_k, pool_v, n_pp=4):
    db, n_pages = page_table.shape
    n_new, w = knew.shape[1], knew.shape[2]
    rows = q.shape[1]
    tail = pool_k.shape[2:]
    kern = functools.partial(_dsa_sample_kernel, n_pp=n_pp, n_pages=n_pages, n_new=n_new,
                             scale=HEAD_DIM ** -0.5)
    row_spec = lambda r, c: pl.BlockSpec((1, r, c), lambda b, g, pt: (b, 0, 0))
    grid_spec = pltpu.PrefetchScalarGridSpec(
        num_scalar_prefetch=1,
        grid=(db, n_pages // n_pp),
        in_specs=[row_spec(rows, HEAD_DIM), row_spec(n_new, w), row_spec(n_new, w),
                  pl.BlockSpec((1, n_new, n_pp * PAGE_SIZE), lambda b, g, pt: (b, 0, g)),
                  pl.BlockSpec((1, n_new, PAGE_SIZE), lambda b, g, pt: (b, 0, n_pages))]
        + _page_specs(n_pp, n_pages, tail, False) + _page_specs(n_pp, n_pages, tail, False),
        out_specs=row_spec(n_new, w),
        scratch_shapes=[pltpu.VMEM((rows, HEAD_DIM), F32), pltpu.VMEM((rows, 1), F32),
                        pltpu.VMEM((rows, 1), F32),
                        pltpu.VMEM((PAGE_SIZE, w), F32), pltpu.VMEM((PAGE_SIZE, w), F32)],
    )
    return pl.pallas_call(
        kern, grid_spec=grid_spec,
        out_shape=jax.ShapeDtypeStruct((db, n_new, w), F32),
        compiler_params=_params(("arbitrary",) * 2),
        name="dsa_sample",
    )(page_table.reshape(-1), q, knew, vnew, bias3, bias3, *([pool_k] * n_pp), *([pool_v] * n_pp))


def _merge_kernel(so_ref, sz_ref, do_ref, dz_ref, gs_ref, gd_ref, ws_ref, wd_ref, o_ref):
    sz = sz_ref[...]
    dz = dz_ref[...]
    a_sb = (so_ref[...] * (sz * jax.nn.sigmoid(sz))).astype(BF16)
    a_ds = (do_ref[...] * (dz * jax.nn.sigmoid(dz))).astype(BF16)
    h_sb = _dot(a_sb, ws_ref[...])
    h_ds = _dot(a_ds, wd_ref[...])
    o_ref[...] = (jax.nn.sigmoid(gs_ref[...]) * h_sb
                  + jax.nn.sigmoid(gd_ref[...]) * h_ds).astype(o_ref.dtype)


def _merge(sb_o, sb_z, ds_o, ds_z, g_sb, g_ds, w_up_sb, w_up_dsa, tm):
    m, w = sb_o.shape
    d = g_sb.shape[1]
    act = pl.BlockSpec((tm, w), lambda i: (i, 0))
    gate = pl.BlockSpec((tm, d), lambda i: (i, 0))
    wgt = pl.BlockSpec((w, d), lambda i: (0, 0))
    return pl.pallas_call(
        _merge_kernel,
        grid=(m // tm,),
        in_specs=[act, act, act, act, gate, gate, wgt, wgt],
        out_specs=gate,
        out_shape=jax.ShapeDtypeStruct((m, d), BF16),
        compiler_params=_params(("arbitrary",)),
        name="merge",
    )(sb_o, sb_z, ds_o, ds_z, g_sb, g_ds, w_up_sb, w_up_dsa)


def _out_kernel(mg_ref, w_ref, x_ref, gate_ref, fg_ref, o_ref):
    y = x_ref[0] + gate_ref[0] * _dot(mg_ref[0], w_ref[...])
    o_ref[0] = y * lax.rsqrt(jnp.mean(y * y, axis=-1, keepdims=True) + RMS_EPS) * fg_ref[...]


def _out(merged3, w_out, x3, gate3, final_g, tr):
    gn, r, d = x3.shape
    mr = gate3.shape[1]
    mblk = 1 if mr == 1 else tr
    gate_map = (lambda b, i: (b, 0, 0)) if mr == 1 else (lambda b, i: (b, i, 0))
    tile = pl.BlockSpec((1, tr, d), lambda b, i: (b, i, 0))
    return pl.pallas_call(
        _out_kernel,
        grid=(gn, r // tr),
        in_specs=[tile, pl.BlockSpec((d, d), lambda b, i: (0, 0)), tile,
                  pl.BlockSpec((1, mblk, d), gate_map),
                  pl.BlockSpec((1, d), lambda b, i: (0, 0))],
        out_specs=tile,
        out_shape=jax.ShapeDtypeStruct((gn, r, d), F32),
        compiler_params=_params(("arbitrary",) * 2),
        name="out_proj",
    )(merged3, w_out, x3, gate3, final_g.reshape(1, d))


def _split_w_in(w_in, d_model):
    w_sb, w_ds = H_SB * HEAD_DIM, H_DSA * HEAD_DIM
    sizes = (("sb_q", w_sb), ("sb_k", w_sb), ("sb_v", w_sb), ("sb_z", w_sb),
             ("ds_q", w_ds), ("ds_k", w_ds), ("ds_v", w_ds), ("ds_z", w_ds),
             ("ix_q", H_IDX * D_IDX), ("ix_k", D_IDX), ("ix_w", H_IDX),
             ("g_sb", d_model), ("g_ds", d_model))
    pieces, off = {}, 0
    for name, n in sizes:
        piece = w_in[:, off:off + n].astype(BF16)
        if n % LANES:
            piece = jnp.pad(piece, ((0, 0), (0, LANES - n % LANES)))
        pieces[name] = piece
        off += n
    assert off == w_in.shape[1]
    return pieces


def _project(xn2, w_pieces, tm):
    return {name: _mm(xn2, w, tm, "proj_" + name) for name, w in w_pieces.items()}


def _head_major_queries(q, n_h):
    b, tq, _ = q.shape
    qh = q.reshape(b, tq, n_h, HEAD_DIM).transpose(0, 2, 1, 3)
    return jnp.pad(qh, ((0, 0), (0, 0), (0, QROWS - tq), (0, 0))).reshape(b, n_h * QROWS, HEAD_DIM)


def kernel(x_prompt, x_sample, c_prompt, c_sample, cache_sb_k, cache_sb_v, cache_dsa_k,
           cache_dsa_v, cache_idx_k, page_table, norm_g, w_ada, b_ada, w_in, w_up_sb,
           w_up_dsa, w_out, final_g):
    bp, t, d = x_prompt.shape
    db, n_new, _ = x_sample.shape
    depth = norm_g.shape[0]
    assert depth == 1, "the final rmsnorm is fused into the layer's output projection"
    n_pool = cache_sb_k.shape[1]
    n_pages = page_table.shape[1]
    past_len = n_pages * PAGE_SIZE
    topk_p = min(TOPK_MAX, t // 4)
    topk_s = min(TOPK_MAX, (past_len + n_new) // 4)
    w_sb, w_ds = H_SB * HEAD_DIM, H_DSA * HEAD_DIM

    xp, xs = x_prompt, x_sample
    n_c = bp + db
    c_all = jnp.concatenate([c_prompt, c_sample, jnp.zeros((-n_c % 16, d), F32)], axis=0)
    rows = [[] for _ in range(10)]
    for l in range(depth):
        mod = _ada_mod(c_all, w_ada[l], b_ada[l])
        mod_p = mod[:bp].reshape(bp, 1, 3 * d)
        mod_s = jnp.repeat(mod[bp:n_c], n_new, axis=0).reshape(1, db * n_new, 3 * d)
        w_pieces = _split_w_in(w_in[l], d)
        wu_sb, wu_ds, wo = w_up_sb[l].astype(BF16), w_up_dsa[l].astype(BF16), w_out[l].astype(BF16)

        xn_p = _xn(xp, norm_g[l], mod_p[:, :, d:2 * d], mod_p[:, :, :d], tr=512)
        p = _project(xn_p.reshape(bp * t, d), w_pieces, tm=1024)
        r3 = lambda a: a.reshape(bp, t, a.shape[-1])
        sb_o = _sb_prompt(r3(p["sb_q"]), r3(p["sb_k"]), r3(p["sb_v"]))
        bias = _idx_prompt(r3(p["ix_q"]), r3(p["ix_w"]), r3(p["ix_k"]), topk_p)
        ds_o = _dsa_prompt(r3(p["ds_q"]), r3(p["ds_k"]), r3(p["ds_v"]), bias)
        merged = _merge(sb_o.reshape(bp * t, w_sb), p["sb_z"], ds_o.reshape(bp * t, w_ds), p["ds_z"],
                        p["g_sb"], p["g_ds"], wu_sb, wu_ds, tm=256)
        xp_new = _out(merged.reshape(bp, t, d), wo, xp, mod_p[:, :, 2 * d:], final_g, tr=256)

        m_s = db * n_new
        xn_s = _xn(xs.reshape(1, m_s, d), norm_g[l], mod_s[:, :, d:2 * d], mod_s[:, :, :d], tr=m_s)
        s = _project(xn_s.reshape(m_s, d), w_pieces, tm=m_s)
        s3 = lambda a: a.reshape(db, n_new, a.shape[-1])
        sb_os = _sb_sample(page_table, _head_major_queries(s3(s["sb_q"]), H_SB),
                           s3(s["sb_k"]), s3(s["sb_v"]), cache_sb_k[l], cache_sb_v[l])
        wcol = (s3(s["ix_w"])[:, :, :H_IDX] * ((D_IDX ** -0.5) * (H_IDX ** -0.5))
                ).reshape(db, n_new * H_IDX, 1)
        scores = _idx_sample(page_table, s["ix_q"].reshape(db, n_new * H_IDX, D_IDX), wcol,
                             s3(s["ix_k"]), cache_idx_k[l])
        n_sc = scores.shape[-1]
        bias_s = _topk_mask(scores.reshape(m_s, n_sc), n_new, past_len, topk_s)
        ds_os = _dsa_sample(page_table, _head_major_queries(s3(s["ds_q"]), H_DSA),
                            s3(s["ds_k"]), s3(s["ds_v"]), bias_s.reshape(db, n_new, n_sc),
                            cache_dsa_k[l], cache_dsa_v[l])
        merged_s = _merge(sb_os.reshape(m_s, w_sb), s["sb_z"], ds_os.reshape(m_s, w_ds), s["ds_z"],
                          s["g_sb"], s["g_ds"], wu_sb, wu_ds, tm=m_s)
        xs_new = _out(merged_s.reshape(1, m_s, d), wo, xs.reshape(1, m_s, d), mod_s[:, :, 2 * d:],
                      final_g, tr=m_s)

        for lst, r in zip(rows, (p["sb_k"].reshape(bp, t, H_SB, HEAD_DIM),
                                 p["sb_v"].reshape(bp, t, H_SB, HEAD_DIM),
                                 p["ds_k"].reshape(bp, t, H_DSA, HEAD_DIM),
                                 p["ds_v"].reshape(bp, t, H_DSA, HEAD_DIM),
                                 p["ix_k"].reshape(bp, t, D_IDX),
                                 s["sb_k"].reshape(db, n_new, H_SB, HEAD_DIM),
                                 s["sb_v"].reshape(db, n_new, H_SB, HEAD_DIM),
                                 s["ds_k"].reshape(db, n_new, H_DSA, HEAD_DIM),
                                 s["ds_v"].reshape(db, n_new, H_DSA, HEAD_DIM),
                                 s["ix_k"].reshape(db, n_new, D_IDX))):
            lst.append(r)
        xp, xs = xp_new, xs_new.reshape(db, n_new, d)

    caches = [r[0][None] if depth == 1 else jnp.stack(r, axis=0) for r in rows]
    return (xp, xs, *caches)
```

```python
import functools

import jax
import jax.numpy as jnp
from jax import lax
from jax.experimental import pallas as pl
from jax.experimental.pallas import tpu as pltpu

F32 = jnp.float32
BF16 = jnp.bfloat16

HEAD_DIM = 128
H_SB = 8
H_DSA = 8
H_IDX = 16
D_IDX = 128
PAGE_SIZE = 128
TOPK_MAX = 256
RMS_EPS = 1e-6
NEG_INF = -1e30
BISECT_ITERS = 32
LANES = 128
VMEM_MB = 56


def _params(sem, vmem_mb=VMEM_MB):
    return pltpu.CompilerParams(dimension_semantics=sem, vmem_limit_bytes=vmem_mb << 20)


def _div_pow2(x, n):
    assert n & (n - 1) == 0
    return x >> (n.bit_length() - 1)


def _mod_pow2(x, n):
    assert n & (n - 1) == 0
    return x & (n - 1)


def _dot(a, b):
    return jnp.dot(a, b, preferred_element_type=F32)


def _dot_nt(a, b):
    return lax.dot_general(a, b, (((1,), (1,)), ((), ())), preferred_element_type=F32)


def _mod_kernel(c_ref, w_ref, b_ref, o_ref):
    c = c_ref[...]
    a = (c * jax.nn.sigmoid(c)).astype(BF16)
    o_ref[...] = _dot(a, w_ref[...].astype(BF16)) + b_ref[...]


def _ada_mod(c, w_ada, b_ada):
    r, d = c.shape
    n = w_ada.shape[1]
    tn = 512
    return pl.pallas_call(
        _mod_kernel,
        grid=(n // tn,),
        in_specs=[pl.BlockSpec((r, d), lambda j: (0, 0)),
                  pl.BlockSpec((d, tn), lambda j: (0, j)),
                  pl.BlockSpec((1, tn), lambda j: (0, j))],
        out_specs=pl.BlockSpec((r, tn), lambda j: (0, j)),
        out_shape=jax.ShapeDtypeStruct((r, n), F32),
        compiler_params=_params(("arbitrary",)),
        name="ada_mod",
    )(c, w_ada, b_ada.reshape(1, n))


def _xn_kernel(x_ref, g_ref, sc_ref, sh_ref, o_ref):
    x = x_ref[0]
    y = x * lax.rsqrt(jnp.mean(x * x, axis=-1, keepdims=True) + RMS_EPS) * g_ref[...]
    o_ref[0] = (y * (1.0 + sc_ref[0]) + sh_ref[0]).astype(o_ref.dtype)


def _xn(x3, g, scale3, shift3, tr):
    gn, r, d = x3.shape
    mr = scale3.shape[1]
    mblk = 1 if mr == 1 else tr
    mod_map = (lambda b, i: (b, 0, 0)) if mr == 1 else (lambda b, i: (b, i, 0))
    return pl.pallas_call(
        _xn_kernel,
        grid=(gn, r // tr),
        in_specs=[pl.BlockSpec((1, tr, d), lambda b, i: (b, i, 0)),
                  pl.BlockSpec((1, d), lambda b, i: (0, 0)),
                  pl.BlockSpec((1, mblk, d), mod_map),
                  pl.BlockSpec((1, mblk, d), mod_map)],
        out_specs=pl.BlockSpec((1, tr, d), lambda b, i: (b, i, 0)),
        out_shape=jax.ShapeDtypeStruct((gn, r, d), BF16),
        compiler_params=_params(("arbitrary", "arbitrary")),
        name="xn",
    )(x3, g.reshape(1, d), scale3, shift3)


def _mm_kernel(x_ref, w_ref, o_ref):
    o_ref[...] = _dot(x_ref[...], w_ref[...]).astype(o_ref.dtype)


def _mm(x, w, tm, name):
    m, k = x.shape
    n = w.shape[1]
    tn = min(n, 1024)
    return pl.pallas_call(
        _mm_kernel,
        grid=(m // tm, n // tn),
        in_specs=[pl.BlockSpec((tm, k), lambda i, j: (i, 0)),
                  pl.BlockSpec((k, tn), lambda i, j: (0, j))],
        out_specs=pl.BlockSpec((tm, tn), lambda i, j: (i, j)),
        out_shape=jax.ShapeDtypeStruct((m, n), F32),
        compiler_params=_params(("arbitrary", "arbitrary")),
        name=name,
    )(x, w)


def _suffix_matrix(n):
    j = lax.broadcasted_iota(jnp.int32, (n, n), 0)
    s = lax.broadcasted_iota(jnp.int32, (n, n), 1)
    return jnp.where(j > s, 1.0, 0.0).astype(BF16)


def _stick_blocks(zs, mask, run, umat):
    n, r = len(zs), zs[0].shape[0]
    z = zs[0] if n == 1 else jnp.concatenate(zs, axis=0)
    sp = jnp.log(1.0 + jnp.exp(-jnp.abs(z)))
    log_b = jnp.minimum(z, 0.0) - sp
    l = jnp.minimum(-z, 0.0) - sp
    if mask is not None:
        l = jnp.where(mask, l, 0.0)
    l_hi = l.astype(BF16)
    l_lo = (l - l_hi.astype(F32)).astype(BF16)
    local = _dot(l_hi, umat) + _dot(l_lo, umat)
    tot = jnp.sum(l, axis=-1, keepdims=True)
    carries = []
    for p in range(n):
        carries.append(run)
        run = run + tot[p * r:(p + 1) * r]
    carry = carries[0] if n == 1 else jnp.concatenate(carries, axis=0)
    a = jnp.exp(log_b + local + carry)
    if mask is not None:
        a = jnp.where(mask, a, 0.0)
    return [a[p * r:(p + 1) * r] for p in range(n)], run


def _sb_prompt_kernel(q_ref, k_ref, v_ref, o_ref, *, tq, scale):
    i = pl.program_id(2)
    qb = q_ref[0].astype(BF16)
    umat = _suffix_matrix(tq)
    row = lax.broadcasted_iota(jnp.int32, (tq, tq), 0)
    col = lax.broadcasted_iota(jnp.int32, (tq, tq), 1)

    def block(j, mask, acc, run):
        ks = pl.multiple_of(j * tq, tq)
        kb = k_ref[0, pl.ds(ks, tq), :].astype(BF16)
        vb = v_ref[0, pl.ds(ks, tq), :].astype(BF16)
        (a,), run = _stick_blocks([_dot_nt(qb, kb) * scale], mask, run, umat)
        return acc + _dot(a.astype(BF16), vb), run

    carry = block(i, col < row, jnp.zeros((tq, HEAD_DIM), F32), jnp.zeros((tq, 1), F32))
    acc, _ = lax.fori_loop(0, i, lambda jj, c: block(i - 1 - jj, None, *c), carry)
    o_ref[0] = acc


def _sb_prompt(q3, k3, v3, tq=256):
    b, t, w = q3.shape
    nh = w // HEAD_DIM
    kern = functools.partial(_sb_prompt_kernel, tq=tq, scale=HEAD_DIM ** -0.5)
    return pl.pallas_call(
        kern,
        grid=(b, nh, t // tq),
        in_specs=[pl.BlockSpec((1, tq, HEAD_DIM), lambda bi, h, i: (bi, i, h)),
                  pl.BlockSpec((1, t, HEAD_DIM), lambda bi, h, i: (bi, 0, h)),
                  pl.BlockSpec((1, t, HEAD_DIM), lambda bi, h, i: (bi, 0, h))],
        out_specs=pl.BlockSpec((1, tq, HEAD_DIM), lambda bi, h, i: (bi, i, h)),
        out_shape=jax.ShapeDtypeStruct((b, t, w), F32),
        compiler_params=_params(("arbitrary",) * 3),
        name="sb_prompt",
    )(q3, k3, v3)


def _topk_bias(score, valid, k):
    big = 3e38
    lo = jnp.min(jnp.where(valid, score, big), axis=-1, keepdims=True)
    hi = jnp.max(jnp.where(valid, score, -big), axis=-1, keepdims=True)
    msc = jnp.where(valid, score, -big)

    def body(_, carry):
        lo, hi = carry
        mid = 0.5 * (lo + hi)
        cnt = jnp.sum(jnp.where(msc >= mid, 1.0, 0.0), axis=-1, keepdims=True)
        ge = cnt >= k
        return jnp.where(ge, mid, lo), jnp.where(ge, hi, mid)

    lo, _ = lax.fori_loop(0, BISECT_ITERS, body, (lo, hi))
    return jnp.where(msc >= lo, 0.0, NEG_INF)


def _idx_prompt_kernel(iq_ref, iw_ref, ik_ref, o_ref, *, tq, topk, wscale):
    i = pl.program_id(1)
    t = ik_ref.shape[1]
    ik = ik_ref[0].astype(BF16)
    w = iw_ref[0] * wscale
    score = jnp.zeros((tq, t), F32)
    for h in range(H_IDX):
        s = _dot_nt(iq_ref[0, :, h * D_IDX:(h + 1) * D_IDX].astype(BF16), ik)
        score = score + w[:, h:h + 1] * jnp.maximum(s, 0.0)
    qpos = lax.broadcasted_iota(jnp.int32, (tq, t), 0) + i * tq
    kpos = lax.broadcasted_iota(jnp.int32, (tq, t), 1)
    o_ref[0] = _topk_bias(score, kpos <= qpos, topk).astype(o_ref.dtype)


def _idx_prompt(iq3, iw3, ik3, topk, tq=256):
    b, t, _ = iq3.shape
    kern = functools.partial(_idx_prompt_kernel, tq=tq, topk=float(topk),
                             wscale=(D_IDX ** -0.5) * (H_IDX ** -0.5))
    return pl.pallas_call(
        kern,
        grid=(b, t // tq),
        in_specs=[pl.BlockSpec((1, tq, H_IDX * D_IDX), lambda bi, i: (bi, i, 0)),
                  pl.BlockSpec((1, tq, LANES), lambda bi, i: (bi, i, 0)),
                  pl.BlockSpec((1, t, D_IDX), lambda bi, i: (bi, 0, 0))],
        out_specs=pl.BlockSpec((1, tq, t), lambda bi, i: (bi, i, 0)),
        out_shape=jax.ShapeDtypeStruct((b, t, t), BF16),
        compiler_params=_params(("arbitrary",) * 2),
        name="idx_prompt",
    )(iq3, iw3, ik3)


def _dsa_prompt_kernel(q_ref, k_ref, v_ref, bias_ref, o_ref, *, tq, scale):
    h = pl.program_id(1)
    i = pl.program_id(2)
    t = k_ref.shape[1]
    s = _dot_nt(q_ref[0].astype(BF16), k_ref[0].astype(BF16)) * scale
    qpos = lax.broadcasted_iota(jnp.int32, (tq, t), 0) + i * tq
    kpos = lax.broadcasted_iota(jnp.int32, (tq, t), 1)
    slope = jnp.exp2(-(jnp.zeros((1, 1), F32) + (h + 1).astype(F32)))
    s = s - slope * (qpos - kpos).astype(F32) + bias_ref[0].astype(F32)
    m = jnp.max(s, axis=-1, keepdims=True)
    p = jnp.exp(s - m)
    l = jnp.sum(p, axis=-1, keepdims=True)
    o_ref[0] = _dot(p.astype(BF16), v_ref[0].astype(BF16)) / l


def _dsa_prompt(q3, k3, v3, bias, tq=256):
    b, t, w = q3.shape
    nh = w // HEAD_DIM
    kern = functools.partial(_dsa_prompt_kernel, tq=tq, scale=HEAD_DIM ** -0.5)
    return pl.pallas_call(
        kern,
        grid=(b, nh, t // tq),
        in_specs=[pl.BlockSpec((1, tq, HEAD_DIM), lambda bi, h, i: (bi, i, h)),
                  pl.BlockSpec((1, t, HEAD_DIM), lambda bi, h, i: (bi, 0, h)),
                  pl.BlockSpec((1, t, HEAD_DIM), lambda bi, h, i: (bi, 0, h)),
                  pl.BlockSpec((1, tq, t), lambda bi, h, i: (bi, i, 0))],
        out_specs=pl.BlockSpec((1, tq, HEAD_DIM), lambda bi, h, i: (bi, i, h)),
        out_shape=jax.ShapeDtypeStruct((b, t, w), F32),
        compiler_params=_params(("arbitrary",) * 3),
        name="dsa_prompt",
    )(q3, k3, v3, bias)


QROWS = 8


def _page_specs(n_per_step, n_pages, page_shape, reverse):
    specs = []
    for p in range(n_per_step):
        def imap(b, g, pt, p=p):
            j = jnp.minimum(g * n_per_step + p, n_pages - 1)
            if reverse:
                j = n_pages - 1 - j
            return (pt[b * n_pages + j],) + (0,) * len(page_shape)
        specs.append(pl.BlockSpec((1,) + tuple(page_shape), imap))
    return specs


def _pool_rows(pool):
    return pool.reshape(pool.shape[0], pool.shape[1] * pool.shape[2], pool.shape[3])


def _paged_heads(ref, n_h):
    return lambda h: ref[0, pl.ds(h, PAGE_SIZE, stride=n_h), :]


def _flat_heads(ref):
    return lambda h: ref[:, h * HEAD_DIM:(h + 1) * HEAD_DIM]


def _heads_qk(q, get_k, n_h):
    return jnp.concatenate(
        [_dot_nt(q[h * QROWS:(h + 1) * QROWS].astype(BF16), get_k(h).astype(BF16))
         for h in range(n_h)], axis=0)


def _heads_pv(p, get_v, n_h):
    return jnp.concatenate(
        [_dot(p[h * QROWS:(h + 1) * QROWS].astype(BF16), get_v(h).astype(BF16))
         for h in range(n_h)], axis=0)


def _store_heads(o_ref, acc, n_new, n_h):
    for h in range(n_h):
        o_ref[0, :, h * HEAD_DIM:(h + 1) * HEAD_DIM] = acc[h * QROWS:h * QROWS + n_new, :]


def _fill_new_rows(pad_ref, new_ref, n_new):
    pad_ref[...] = jnp.zeros_like(pad_ref)
    pad_ref[0:n_new, :] = new_ref[0]


def _sb_sample_kernel(pt_ref, q_ref, knew_ref, vnew_ref, *rest, n_pp, n_pages, n_new, scale):
    k_refs = rest[:n_pp]
    v_refs = rest[n_pp:2 * n_pp]
    o_ref, acc_ref, run_ref, kpad_ref, vpad_ref = rest[2 * n_pp:]
    g = pl.program_id(1)
    past_len = n_pages * PAGE_SIZE
    rows = q_ref.shape[1]
    q = q_ref[0]
    umat = _suffix_matrix(PAGE_SIZE)
    col = lax.broadcasted_iota(jnp.int32, (rows, PAGE_SIZE), 1)
    qi = _mod_pow2(lax.broadcasted_iota(jnp.int32, (rows, PAGE_SIZE), 0), QROWS)
    qpos = past_len + jnp.minimum(qi, n_new - 1)

    def visit(get_ks, get_vs, mask, acc, run):
        zs = [_heads_qk(q, get_k, H_SB) * scale for get_k in get_ks]
        a_blocks, run = _stick_blocks(zs, mask, run, umat)
        for a, get_v in zip(a_blocks, get_vs):
            acc = acc + _heads_pv(a, get_v, H_SB)
        return acc, run

    @pl.when(g == 0)
    def _():
        _fill_new_rows(kpad_ref, knew_ref, n_new)
        _fill_new_rows(vpad_ref, vnew_ref, n_new)
        acc, run = visit([_flat_heads(kpad_ref)], [_flat_heads(vpad_ref)], (col + past_len) < qpos,
                         jnp.zeros(acc_ref.shape, F32), jnp.zeros(run_ref.shape, F32))
        acc_ref[...] = acc
        run_ref[...] = run

    acc, run = visit([_paged_heads(r, H_SB) for r in k_refs], [_paged_heads(r, H_SB) for r in v_refs],
                     None, acc_ref[...], run_ref[...])
    acc_ref[...] = acc
    run_ref[...] = run

    @pl.when(g == pl.num_programs(1) - 1)
    def _():
        _store_heads(o_ref, acc, n_new, H_SB)


def _sb_sample(page_table, q, knew, vnew, pool_k, pool_v, n_pp=8):
    db, n_pages = page_table.shape
    n_pp = min(n_pp, n_pages)
    n_new, w = knew.shape[1], knew.shape[2]
    rows = q.shape[1]
    pool_k, pool_v = _pool_rows(pool_k), _pool_rows(pool_v)
    tail = pool_k.shape[1:]
    kern = functools.partial(_sb_sample_kernel, n_pp=n_pp, n_pages=n_pages, n_new=n_new,
                             scale=HEAD_DIM ** -0.5)
    row_spec = lambda r, c: pl.BlockSpec((1, r, c), lambda b, g, pt: (b, 0, 0))
    grid_spec = pltpu.PrefetchScalarGridSpec(
        num_scalar_prefetch=1,
        grid=(db, n_pages // n_pp),
        in_specs=[row_spec(rows, HEAD_DIM), row_spec(n_new, w), row_spec(n_new, w)]
        + _page_specs(n_pp, n_pages, tail, True) + _page_specs(n_pp, n_pages, tail, True),
        out_specs=row_spec(n_new, w),
        scratch_shapes=[pltpu.VMEM((rows, HEAD_DIM), F32), pltpu.VMEM((rows, 1), F32),
                        pltpu.VMEM((PAGE_SIZE, w), F32), pltpu.VMEM((PAGE_SIZE, w), F32)],
    )
    return pl.pallas_call(
        kern, grid_spec=grid_spec,
        out_shape=jax.ShapeDtypeStruct((db, n_new, w), F32),
        compiler_params=_params(("arbitrary",) * 2),
        name="sb_sample",
    )(page_table.reshape(-1), q, knew, vnew, *([pool_k] * n_pp), *([pool_v] * n_pp))


def _idx_sample_kernel(pt_ref, iq_ref, w_ref, iknew_ref, *rest, n_pp, n_new):
    ik_refs = rest[:n_pp]
    o_ref, pad_ref = rest[n_pp:]
    g = pl.program_id(1)
    iq = iq_ref[0].astype(BF16)
    w = w_ref[0]

    def page_scores(ikpage):
        s = jnp.maximum(_dot_nt(iq, ikpage.astype(BF16)), 0.0) * w
        return jnp.sum(s.reshape(n_new, H_IDX, PAGE_SIZE), axis=1)

    @pl.when(g < pl.num_programs(1) - 1)
    def _():
        for p in range(n_pp):
            o_ref[0, :, p * PAGE_SIZE:(p + 1) * PAGE_SIZE] = page_scores(ik_refs[p][0])

    @pl.when(g == pl.num_programs(1) - 1)
    def _():
        pad_ref[...] = jnp.zeros_like(pad_ref)
        pad_ref[0:n_new, :] = iknew_ref[0]
        o_ref[0] = jnp.zeros(o_ref.shape[1:], F32)
        o_ref[0, :, 0:PAGE_SIZE] = page_scores(pad_ref[...])


def _idx_sample(page_table, iq, wcol, iknew, pool_ik, n_pp=16):
    db, n_pages = page_table.shape
    n_pp = min(n_pp, n_pages)
    n_new = iknew.shape[1]
    rows = iq.shape[1]
    n_groups = n_pages // n_pp
    kern = functools.partial(_idx_sample_kernel, n_pp=n_pp, n_new=n_new)
    grid_spec = pltpu.PrefetchScalarGridSpec(
        num_scalar_prefetch=1,
        grid=(db, n_groups + 1),
        in_specs=[pl.BlockSpec((1, rows, D_IDX), lambda b, g, pt: (b, 0, 0)),
                  pl.BlockSpec((1, rows, 1), lambda b, g, pt: (b, 0, 0)),
                  pl.BlockSpec((1, n_new, D_IDX), lambda b, g, pt: (b, 0, 0))]
        + _page_specs(n_pp, n_pages, (PAGE_SIZE, D_IDX), False),
        out_specs=pl.BlockSpec((1, n_new, n_pp * PAGE_SIZE), lambda b, g, pt: (b, 0, g)),
        scratch_shapes=[pltpu.VMEM((PAGE_SIZE, D_IDX), F32)],
    )
    return pl.pallas_call(
        kern, grid_spec=grid_spec,
        out_shape=jax.ShapeDtypeStruct((db, n_new, (n_groups + 1) * n_pp * PAGE_SIZE), F32),
        compiler_params=_params(("arbitrary",) * 2),
        name="idx_sample",
    )(page_table.reshape(-1), iq, wcol, iknew, *([pool_ik] * n_pp))


def _topk_mask_kernel(s_ref, o_ref, *, n_new, past_len, topk):
    shape = s_ref.shape
    qpos = past_len + _mod_pow2(lax.broadcasted_iota(jnp.int32, shape, 0), n_new)
    kpos = lax.broadcasted_iota(jnp.int32, shape, 1)
    o_ref[...] = _topk_bias(s_ref[...], kpos <= qpos, topk)


def _topk_mask(scores2, n_new, past_len, topk):
    kern = functools.partial(_topk_mask_kernel, n_new=n_new, past_len=past_len, topk=float(topk))
    return pl.pallas_call(
        kern,
        grid=(1,),
        in_specs=[pl.BlockSpec(scores2.shape, lambda i: (0, 0))],
        out_specs=pl.BlockSpec(scores2.shape, lambda i: (0, 0)),
        out_shape=jax.ShapeDtypeStruct(scores2.shape, F32),
        compiler_params=_params(("arbitrary",)),
        name="topk_mask",
    )(scores2)


def _dsa_sample_kernel(pt_ref, q_ref, knew_ref, vnew_ref, bias_ref, bias_new_ref, *rest,
                       n_pp, n_pages, n_new, scale):
    k_refs = rest[:n_pp]
    v_refs = rest[n_pp:2 * n_pp]
    o_ref, acc_ref, m_ref, l_ref, kpad_ref, vpad_ref = rest[2 * n_pp:]
    g = pl.program_id(1)
    past_len = n_pages * PAGE_SIZE
    rows = q_ref.shape[1]
    q = q_ref[0]

    def visit(get_ks, get_vs, bias4, kpos0, acc, m, l):
        width = len(get_ks) * PAGE_SIZE
        rid = lax.broadcasted_iota(jnp.int32, (rows, width), 0)
        kpos = lax.broadcasted_iota(jnp.int32, (rows, width), 1) + kpos0
        qi = _mod_pow2(rid, QROWS)
        slope = jnp.exp2(-(_div_pow2(rid, QROWS) + 1).astype(F32))
        qpos = past_len + jnp.minimum(qi, n_new - 1)
        bias = jnp.zeros((rows, width), F32)
        for r in range(n_new):
            bias = jnp.where(qi == r, bias4[r:r + 1, :], bias)
        s = [_heads_qk(q, get_k, H_DSA) for get_k in get_ks]
        s = (s[0] if len(s) == 1 else jnp.concatenate(s, axis=1)) * scale
        s = s - slope * (qpos - kpos).astype(F32) + bias
        m_new = jnp.maximum(m, jnp.max(s, axis=-1, keepdims=True))
        alpha = jnp.exp(m - m_new)
        p = jnp.exp(s - m_new)
        l = alpha * l + jnp.sum(p, axis=-1, keepdims=True)
        acc = alpha * acc
        for j, get_v in enumerate(get_vs):
            acc = acc + _heads_pv(p[:, j * PAGE_SIZE:(j + 1) * PAGE_SIZE], get_v, H_DSA)
        return acc, m_new, l

    @pl.when(g == 0)
    def _():
        _fill_new_rows(kpad_ref, knew_ref, n_new)
        _fill_new_rows(vpad_ref, vnew_ref, n_new)
        acc, m, l = visit([_flat_heads(kpad_ref)], [_flat_heads(vpad_ref)], bias_new_ref[0],
                          past_len, jnp.zeros(acc_ref.shape, F32),
                          jnp.full(m_ref.shape, NEG_INF, F32), jnp.zeros(l_ref.shape, F32))
        acc_ref[...] = acc
        m_ref[...] = m
        l_ref[...] = l

    acc, m, l = visit([_paged_heads(r, H_DSA) for r in k_refs], [_paged_heads(r, H_DSA) for r in v_refs],
                      bias_ref[0], g * (n_pp * PAGE_SIZE), acc_ref[...], m_ref[...], l_ref[...])
    acc_ref[...] = acc
    m_ref[...] = m
    l_ref[...] = l

    @pl.when(g == pl.num_programs(1) - 1)
    def _():
        _store_heads(o_ref, acc / l, n_new, H_DSA)


def _dsa_sample(page_table, q, knew, vnew, bias3, pool_k, pool_v, n_pp=8):
    db, n_pages = page_table.shape
    n_pp = min(n_pp, n_pages)
    n_new, w = knew.shape[1], knew.shape[2]
    rows = q.shape[1]
    pool_k, pool_v = _pool_rows(pool_k), _pool_rows(pool_v)
    tail = pool_k.shape[1:]
    kern = functools.partial(_dsa_sample_kernel, n_pp=n_pp, n_pages=n_pages, n_new=n_new,
                             scale=HEAD_DIM ** -0.5)
    row_spec = lambda r, c: pl.BlockSpec((1, r, c), lambda b, g, pt: (b, 0, 0))
    grid_spec = pltpu.PrefetchScalarGridSpec(
        num_scalar_prefetch=1,
        grid=(db, n_pages // n_pp),
        in_specs=[row_spec(rows, HEAD_DIM), row_spec(n_new, w), row_spec(n_new, w),
                  pl.BlockSpec((1, n_new, n_pp * PAGE_SIZE), lambda b, g, pt: (b, 0, g)),
                  pl.BlockSpec((1, n_new, PAGE_SIZE), lambda b, g, pt: (b, 0, n_pages))]
        + _page_specs(n_pp, n_pages, tail, False) + _page_specs(n_pp, n_pages, tail, False),
        out_specs=row_spec(n_new, w),
        scratch_shapes=[pltpu.VMEM((rows, HEAD_DIM), F32), pltpu.VMEM((rows, 1), F32),
                        pltpu.VMEM((rows, 1), F32),
                        pltpu.VMEM((PAGE_SIZE, w), F32), pltpu.VMEM((PAGE_SIZE, w), F32)],
    )
    return pl.pallas_call(
        kern, grid_spec=grid_spec,
        out_shape=jax.ShapeDtypeStruct((db, n_new, w), F32),
        compiler_params=_params(("arbitrary",) * 2),
        name="dsa_sample",
    )(page_table.reshape(-1), q, knew, vnew, bias3, bias3, *([pool_k] * n_pp), *([pool_v] * n_pp))


def _merge_kernel(so_ref, sz_ref, do_ref, dz_ref, gs_ref, gd_ref, ws_ref, wd_ref, o_ref):
    sz = sz_ref[...]
    dz = dz_ref[...]
    a_sb = (so_ref[...] * (sz * jax.nn.sigmoid(sz))).astype(BF16)
    a_ds = (do_ref[...] * (dz * jax.nn.sigmoid(dz))).astype(BF16)
    h_sb = _dot(a_sb, ws_ref[...])
    h_ds = _dot(a_ds, wd_ref[...])
    o_ref[...] = (jax.nn.sigmoid(gs_ref[...]) * h_sb
                  + jax.nn.sigmoid(gd_ref[...]) * h_ds).astype(o_ref.dtype)


def _merge(sb_o, sb_z, ds_o, ds_z, g_sb, g_ds, w_up_sb, w_up_dsa, tm):
    m, w = sb_o.shape
    d = g_sb.shape[1]
    act = pl.BlockSpec((tm, w), lambda i: (i, 0))
    gate = pl.BlockSpec((tm, d), lambda i: (i, 0))
    wgt = pl.BlockSpec((w, d), lambda i: (0, 0))
    return pl.pallas_call(
        _merge_kernel,
        grid=(m // tm,),
        in_specs=[act, act, act, act, gate, gate, wgt, wgt],
        out_specs=gate,
        out_shape=jax.ShapeDtypeStruct((m, d), BF16),
        compiler_params=_params(("arbitrary",)),
        name="merge",
    )(sb_o, sb_z, ds_o, ds_z, g_sb, g_ds, w_up_sb, w_up_dsa)


def _out_kernel(mg_ref, w_ref, x_ref, gate_ref, fg_ref, o_ref):
    y = x_ref[0] + gate_ref[0] * _dot(mg_ref[0], w_ref[...])
    o_ref[0] = y * lax.rsqrt(jnp.mean(y * y, axis=-1, keepdims=True) + RMS_EPS) * fg_ref[...]


def _out(merged3, w_out, x3, gate3, final_g, tr):
    gn, r, d = x3.shape
    mr = gate3.shape[1]
    mblk = 1 if mr == 1 else tr
    gate_map = (lambda b, i: (b, 0, 0)) if mr == 1 else (lambda b, i: (b, i, 0))
    tile = pl.BlockSpec((1, tr, d), lambda b, i: (b, i, 0))
    return pl.pallas_call(
        _out_kernel,
        grid=(gn, r // tr),
        in_specs=[tile, pl.BlockSpec((d, d), lambda b, i: (0, 0)), tile,
                  pl.BlockSpec((1, mblk, d), gate_map),
                  pl.BlockSpec((1, d), lambda b, i: (0, 0))],
        out_specs=tile,
        out_shape=jax.ShapeDtypeStruct((gn, r, d), F32),
        compiler_params=_params(("arbitrary",) * 2),
        name="out_proj",
    )(merged3, w_out, x3, gate3, final_g.reshape(1, d))


def _split_w_in(w_in, d_model):
    w_sb, w_ds = H_SB * HEAD_DIM, H_DSA * HEAD_DIM
    sizes = (("sb_q", w_sb), ("sb_k", w_sb), ("sb_v", w_sb), ("sb_z", w_sb),
             ("ds_q", w_ds), ("ds_k", w_ds), ("ds_v", w_ds), ("ds_z", w_ds),
             ("ix_q", H_IDX * D_IDX), ("ix_k", D_IDX), ("ix_w", H_IDX),
             ("g_sb", d_model), ("g_ds", d_model))
    pieces, off = {}, 0
    for name, n in sizes:
        piece = w_in[:, off:off + n].astype(BF16)
        if n % LANES:
            piece = jnp.pad(piece, ((0, 0), (0, LANES - n % LANES)))
        pieces[name] = piece
        off += n
    assert off == w_in.shape[1]
    return pieces


def _project(xn2, w_pieces, tm):
    return {name: _mm(xn2, w, tm, "proj_" + name) for name, w in w_pieces.items()}


def _head_major_queries(q, n_h):
    b, tq, _ = q.shape
    qh = q.reshape(b, tq, n_h, HEAD_DIM).transpose(0, 2, 1, 3)
    return jnp.pad(qh, ((0, 0), (0, 0), (0, QROWS - tq), (0, 0))).reshape(b, n_h * QROWS, HEAD_DIM)


def kernel(x_prompt, x_sample, c_prompt, c_sample, cache_sb_k, cache_sb_v, cache_dsa_k,
           cache_dsa_v, cache_idx_k, page_table, norm_g, w_ada, b_ada, w_in, w_up_sb,
           w_up_dsa, w_out, final_g):
    bp, t, d = x_prompt.shape
    db, n_new, _ = x_sample.shape
    depth = norm_g.shape[0]
    assert depth == 1, "the final rmsnorm is fused into the layer's output projection"
    n_pool = cache_sb_k.shape[1]
    n_pages = page_table.shape[1]
    past_len = n_pages * PAGE_SIZE
    topk_p = min(TOPK_MAX, t // 4)
    topk_s = min(TOPK_MAX, (past_len + n_new) // 4)
    w_sb, w_ds = H_SB * HEAD_DIM, H_DSA * HEAD_DIM

    xp, xs = x_prompt, x_sample
    n_c = bp + db
    c_all = jnp.concatenate([c_prompt, c_sample, jnp.zeros((-n_c % 16, d), F32)], axis=0)
    rows = [[] for _ in range(10)]
    for l in range(depth):
        mod = _ada_mod(c_all, w_ada[l], b_ada[l])
        mod_p = mod[:bp].reshape(bp, 1, 3 * d)
        mod_s = jnp.repeat(mod[bp:n_c], n_new, axis=0).reshape(1, db * n_new, 3 * d)
        w_pieces = _split_w_in(w_in[l], d)
        wu_sb, wu_ds, wo = w_up_sb[l].astype(BF16), w_up_dsa[l].astype(BF16), w_out[l].astype(BF16)

        xn_p = _xn(xp, norm_g[l], mod_p[:, :, d:2 * d], mod_p[:, :, :d], tr=512)
        p = _project(xn_p.reshape(bp * t, d), w_pieces, tm=1024)
        r3 = lambda a: a.reshape(bp, t, a.shape[-1])
        sb_o = _sb_prompt(r3(p["sb_q"]), r3(p["sb_k"]), r3(p["sb_v"]))
        bias = _idx_prompt(r3(p["ix_q"]), r3(p["ix_w"]), r3(p["ix_k"]), topk_p)
        ds_o = _dsa_prompt(r3(p["ds_q"]), r3(p["ds_k"]), r3(p["ds_v"]), bias)
        merged = _merge(sb_o.reshape(bp * t, w_sb), p["sb_z"], ds_o.reshape(bp * t, w_ds), p["ds_z"],
                        p["g_sb"], p["g_ds"], wu_sb, wu_ds, tm=256)
        xp_new = _out(merged.reshape(bp, t, d), wo, xp, mod_p[:, :, 2 * d:], final_g, tr=256)

        m_s = db * n_new
        xn_s = _xn(xs.reshape(1, m_s, d), norm_g[l], mod_s[:, :, d:2 * d], mod_s[:, :, :d], tr=m_s)
        s = _project(xn_s.reshape(m_s, d), w_pieces, tm=m_s)
        s3 = lambda a: a.reshape(db, n_new, a.shape[-1])
        sb_os = _sb_sample(page_table, _head_major_queries(s3(s["sb_q"]), H_SB),
                           s3(s["sb_k"]), s3(s["sb_v"]), cache_sb_k[l], cache_sb_v[l])
        wcol = (s3(s["ix_w"])[:, :, :H_IDX] * ((D_IDX ** -0.5) * (H_IDX ** -0.5))
                ).reshape(db, n_new * H_IDX, 1)
        scores = _idx_sample(page_table, s["ix_q"].reshape(db, n_new * H_IDX, D_IDX), wcol,
                             s3(s["ix_k"]), cache_idx_k[l])
        n_sc = scores.shape[-1]
        bias_s = _topk_mask(scores.reshape(m_s, n_sc), n_new, past_len, topk_s)
        ds_os = _dsa_sample(page_table, _head_major_queries(s3(s["ds_q"]), H_DSA),
                            s3(s["ds_k"]), s3(s["ds_v"]), bias_s.reshape(db, n_new, n_sc),
                            cache_dsa_k[l], cache_dsa_v[l])
        merged_s = _merge(sb_os.reshape(m_s, w_sb), s["sb_z"], ds_os.reshape(m_s, w_ds), s["ds_z"],
                          s["g_sb"], s["g_ds"], wu_sb, wu_ds, tm=m_s)
        xs_new = _out(merged_s.reshape(1, m_s, d), wo, xs.reshape(1, m_s, d), mod_s[:, :, 2 * d:],
                      final_g, tr=m_s)

        for lst, r in zip(rows, (p["sb_k"].reshape(bp, t, H_SB, HEAD_DIM),
                                 p["sb_v"].reshape(bp, t, H_SB, HEAD_DIM),
                                 p["ds_k"].reshape(bp, t, H_DSA, HEAD_DIM),
                                 p["ds_v"].reshape(bp, t, H_DSA, HEAD_DIM),
                                 p["ix_k"].reshape(bp, t, D_IDX),
                                 s["sb_k"].reshape(db, n_new, H_SB, HEAD_DIM),
                                 s["sb_v"].reshape(db, n_new, H_SB, HEAD_DIM),
                                 s["ds_k"].reshape(db, n_new, H_DSA, HEAD_DIM),
                                 s["ds_v"].reshape(db, n_new, H_DSA, HEAD_DIM),
                                 s["ix_k"].reshape(db, n_new, D_IDX))):
            lst.append(r)
        xp, xs = xp_new, xs_new.reshape(db, n_new, d)

    caches = [r[0][None] if depth == 1 else jnp.stack(r, axis=0) for r in rows]
    return (xp, xs, *caches)
```

```python
import functools

import jax
import jax.numpy as jnp
from jax import lax
from jax.experimental import pallas as pl
from jax.experimental.pallas import tpu as pltpu

F32 = jnp.float32
BF16 = jnp.bfloat16

HEAD_DIM = 128
H_SB = 8
H_DSA = 8
H_IDX = 16
D_IDX = 128
PAGE_SIZE = 128
TOPK_MAX = 256
RMS_EPS = 1e-6
NEG_INF = -1e30
BISECT_ITERS = 32
LANES = 128
VMEM_MB = 56


def _params(sem, vmem_mb=VMEM_MB):
    return pltpu.CompilerParams(dimension_semantics=sem, vmem_limit_bytes=vmem_mb << 20)


def _div_pow2(x, n):
    assert n & (n - 1) == 0
    return x >> (n.bit_length() - 1)


def _mod_pow2(x, n):
    assert n & (n - 1) == 0
    return x & (n - 1)


def _dot(a, b):
    return jnp.dot(a, b, preferred_element_type=F32)


def _dot_nt(a, b):
    return lax.dot_general(a, b, (((1,), (1,)), ((), ())), preferred_element_type=F32)


def _mod_kernel(c_ref, w_ref, b_ref, o_ref):
    c = c_ref[...]
    a = (c * jax.nn.sigmoid(c)).astype(BF16)
    o_ref[...] = _dot(a, w_ref[...].astype(BF16)) + b_ref[...]


def _ada_mod(c, w_ada, b_ada):
    r, d = c.shape
    n = w_ada.shape[1]
    tn = 512
    return pl.pallas_call(
        _mod_kernel,
        grid=(n // tn,),
        in_specs=[pl.BlockSpec((r, d), lambda j: (0, 0)),
                  pl.BlockSpec((d, tn), lambda j: (0, j)),
                  pl.BlockSpec((1, tn), lambda j: (0, j))],
        out_specs=pl.BlockSpec((r, tn), lambda j: (0, j)),
        out_shape=jax.ShapeDtypeStruct((r, n), F32),
        compiler_params=_params(("arbitrary",)),
        name="ada_mod",
    )(c, w_ada, b_ada.reshape(1, n))


def _xn_kernel(x_ref, g_ref, sc_ref, sh_ref, o_ref):
    x = x_ref[0]
    y = x * lax.rsqrt(jnp.mean(x * x, axis=-1, keepdims=True) + RMS_EPS) * g_ref[...]
    o_ref[0] = (y * (1.0 + sc_ref[0]) + sh_ref[0]).astype(o_ref.dtype)


def _xn(x3, g, scale3, shift3, tr):
    gn, r, d = x3.shape
    mr = scale3.shape[1]
    mblk = 1 if mr == 1 else tr
    mod_map = (lambda b, i: (b, 0, 0)) if mr == 1 else (lambda b, i: (b, i, 0))
    return pl.pallas_call(
        _xn_kernel,
        grid=(gn, r // tr),
        in_specs=[pl.BlockSpec((1, tr, d), lambda b, i: (b, i, 0)),
                  pl.BlockSpec((1, d), lambda b, i: (0, 0)),
                  pl.BlockSpec((1, mblk, d), mod_map),
                  pl.BlockSpec((1, mblk, d), mod_map)],
        out_specs=pl.BlockSpec((1, tr, d), lambda b, i: (b, i, 0)),
        out_shape=jax.ShapeDtypeStruct((gn, r, d), BF16),
        compiler_params=_params(("arbitrary", "arbitrary")),
        name="xn",
    )(x3, g.reshape(1, d), scale3, shift3)


def _mm_kernel(x_ref, w_ref, o_ref):
    o_ref[...] = _dot(x_ref[...], w_ref[...]).astype(o_ref.dtype)


def _mm(x, w, tm, name):
    m, k = x.shape
    n = w.shape[1]
    tn = min(n, 1024)
    return pl.pallas_call(
        _mm_kernel,
        grid=(m // tm, n // tn),
        in_specs=[pl.BlockSpec((tm, k), lambda i, j: (i, 0)),
                  pl.BlockSpec((k, tn), lambda i, j: (0, j))],
        out_specs=pl.BlockSpec((tm, tn), lambda i, j: (i, j)),
        out_shape=jax.ShapeDtypeStruct((m, n), F32),
        compiler_params=_params(("arbitrary", "arbitrary")),
        name=name,
    )(x, w)


def _suffix_matrix(n):
    j = lax.broadcasted_iota(jnp.int32, (n, n), 0)
    s = lax.broadcasted_iota(jnp.int32, (n, n), 1)
    return jnp.where(j > s, 1.0, 0.0).astype(BF16)


def _stick_blocks(zs, mask, run, umat):
    n, r = len(zs), zs[0].shape[0]

    def masked(x):
        if mask is None:
            return x
        if mask.shape[0] == x.shape[0]:
            return jnp.where(mask, x, 0.0)
        return jnp.concatenate([jnp.where(mask, x[:r], 0.0), x[r:]], axis=0)

    z = zs[0] if n == 1 else jnp.concatenate(zs, axis=0)
    sp = jnp.log(1.0 + jnp.exp(-jnp.abs(z)))
    log_b = jnp.minimum(z, 0.0) - sp
    l = masked(jnp.minimum(-z, 0.0) - sp)
    l_hi = l.astype(BF16)
    l_lo = (l - l_hi.astype(F32)).astype(BF16)
    local = _dot(l_hi, umat) + _dot(l_lo, umat)
    tot = jnp.sum(l, axis=-1, keepdims=True)
    carries = []
    for p in range(n):
        carries.append(run)
        run = run + tot[p * r:(p + 1) * r]
    carry = carries[0] if n == 1 else jnp.concatenate(carries, axis=0)
    a = masked(jnp.exp(log_b + local + carry))
    return [a[p * r:(p + 1) * r] for p in range(n)], run


def _sb_prompt_kernel(q_ref, k_ref, v_ref, o_ref, *, tq, scale):
    i = pl.program_id(2)
    qb = q_ref[0].astype(BF16)
    umat = _suffix_matrix(tq)
    row = lax.broadcasted_iota(jnp.int32, (tq, tq), 0)
    col = lax.broadcasted_iota(jnp.int32, (tq, tq), 1)

    for n in range(1, k_ref.shape[1] // tq + 1):
        @pl.when(i == n - 1)
        def _(n=n):
            kb = k_ref[0, 0:n * tq, :].astype(BF16)
            vb = v_ref[0, 0:n * tq, :].astype(BF16)
            z = _dot_nt(qb, kb) * scale
            zs = [z[:, j * tq:(j + 1) * tq] for j in reversed(range(n))]
            a_blocks, _ = _stick_blocks(zs, col < row, jnp.zeros((tq, 1), F32), umat)
            a = a_blocks[0] if n == 1 else jnp.concatenate(a_blocks[::-1], axis=1)
            o_ref[0] = _dot(a.astype(BF16), vb)


def _sb_prompt(q3, k3, v3, tq=256):
    b, t, w = q3.shape
    nh = w // HEAD_DIM
    kern = functools.partial(_sb_prompt_kernel, tq=tq, scale=HEAD_DIM ** -0.5)
    return pl.pallas_call(
        kern,
        grid=(b, nh, t // tq),
        in_specs=[pl.BlockSpec((1, tq, HEAD_DIM), lambda bi, h, i: (bi, i, h)),
                  pl.BlockSpec((1, t, HEAD_DIM), lambda bi, h, i: (bi, 0, h)),
                  pl.BlockSpec((1, t, HEAD_DIM), lambda bi, h, i: (bi, 0, h))],
        out_specs=pl.BlockSpec((1, tq, HEAD_DIM), lambda bi, h, i: (bi, i, h)),
        out_shape=jax.ShapeDtypeStruct((b, t, w), F32),
        compiler_params=_params(("arbitrary",) * 3),
        name="sb_prompt",
    )(q3, k3, v3)


def _topk_bias(score, valid, k):
    big = 3e38
    lo = jnp.min(jnp.where(valid, score, big), axis=-1, keepdims=True)
    hi = jnp.max(jnp.where(valid, score, -big), axis=-1, keepdims=True)
    msc = jnp.where(valid, score, -big)

    def body(_, carry):
        lo, hi = carry
        mid = 0.5 * (lo + hi)
        cnt = jnp.sum(jnp.where(msc >= mid, 1.0, 0.0), axis=-1, keepdims=True)
        ge = cnt >= k
        return jnp.where(ge, mid, lo), jnp.where(ge, hi, mid)

    lo, _ = lax.fori_loop(0, BISECT_ITERS, body, (lo, hi))
    return jnp.where(msc >= lo, 0.0, NEG_INF)


def _idx_prompt_kernel(iq_ref, iw_ref, ik_ref, o_ref, iqs_ref, sc_ref, *, tq, kc, topk, wscale):
    i = pl.program_id(1)
    n_kc = sc_ref.shape[0]
    n_c = lax.div(i * tq + (tq - 1), jnp.int32(kc)) + 1
    big = 3e38
    for h in range(H_IDX):
        iqs_ref[h * tq:(h + 1) * tq, :] = iq_ref[0, :, h * D_IDX:(h + 1) * D_IDX].astype(BF16)
    wt = (iw_ref[0] * wscale).T
    kpos = lax.broadcasted_iota(jnp.int32, (kc, tq), 0)
    qpos = lax.broadcasted_iota(jnp.int32, (kc, tq), 1) + i * tq

    def score_chunk(c, carry):
        lo, hi = carry
        ks = pl.multiple_of(c * kc, kc)
        s_all = _dot_nt(ik_ref[0, pl.ds(ks, kc), :].astype(BF16), iqs_ref[...])
        score = jnp.zeros((kc, tq), F32)
        for h in range(H_IDX):
            score = score + wt[h:h + 1, :] * jnp.maximum(s_all[:, h * tq:(h + 1) * tq], 0.0)
        valid = (kpos + c * kc) <= qpos
        sc_ref[c] = jnp.where(valid, score, -big)
        lo = jnp.minimum(lo, jnp.min(jnp.where(valid, score, big), axis=0, keepdims=True))
        hi = jnp.maximum(hi, jnp.max(jnp.where(valid, score, -big), axis=0, keepdims=True))
        return lo, hi

    lo, hi = lax.fori_loop(0, n_c, score_chunk,
                           (jnp.full((1, tq), big, F32), jnp.full((1, tq), -big, F32)))

    def bisect(_, carry):
        lo, hi = carry
        mid = 0.5 * (lo + hi)

        def count_chunk(c, acc):
            ind = jnp.where(sc_ref[c] >= mid, 1.0, 0.0)
            return acc + jnp.sum(ind.reshape(kc // 8, 8, tq), axis=0)

        acc = lax.fori_loop(0, n_c, count_chunk, jnp.zeros((8, tq), F32))
        ge = jnp.sum(acc, axis=0, keepdims=True) >= topk
        return jnp.where(ge, mid, lo), jnp.where(ge, hi, mid)

    lo, _ = lax.fori_loop(0, BISECT_ITERS, bisect, (lo, hi))

    def write_chunk(c, _):
        o_ref[0, c] = jnp.where(sc_ref[c] >= lo, 0.0, NEG_INF).T.astype(o_ref.dtype)
        return 0

    def fill_chunk(c, _):
        o_ref[0, c] = jnp.full((tq, kc), NEG_INF, o_ref.dtype)
        return 0

    lax.fori_loop(0, n_c, write_chunk, 0)
    lax.fori_loop(n_c, n_kc, fill_chunk, 0)


def _idx_prompt(iq3, iw3, ik3, topk, tq=256, kc=256):
    b, t, _ = iq3.shape
    n_kc = t // kc
    kern = functools.partial(_idx_prompt_kernel, tq=tq, kc=kc, topk=float(topk),
                             wscale=(D_IDX ** -0.5) * (H_IDX ** -0.5))
    return pl.pallas_call(
        kern,
        grid=(b, t // tq),
        in_specs=[pl.BlockSpec((1, tq, H_IDX * D_IDX), lambda bi, i: (bi, i, 0)),
                  pl.BlockSpec((1, tq, LANES), lambda bi, i: (bi, i, 0)),
                  pl.BlockSpec((1, t, D_IDX), lambda bi, i: (bi, 0, 0))],
        out_specs=pl.BlockSpec((1, n_kc, tq, kc), lambda bi, i: (bi, 0, i, 0)),
        out_shape=jax.ShapeDtypeStruct((b, n_kc, t, kc), BF16),
        scratch_shapes=[pltpu.VMEM((H_IDX * tq, D_IDX), BF16), pltpu.VMEM((n_kc, kc, tq), F32)],
        compiler_params=_params(("arbitrary",) * 2),
        name="idx_prompt",
    )(iq3, iw3, ik3)


def _dsa_prompt_kernel(q_ref, k_ref, v_ref, bias_ref, o_ref, *, tq, span, scale):
    h = pl.program_id(1)
    i = pl.program_id(2)
    n_kc, kc = bias_ref.shape[1], bias_ref.shape[3]
    qb = q_ref[0].astype(BF16)
    slope = jnp.exp2(-(jnp.zeros((1, 1), F32) + (h + 1).astype(F32)))
    n_c = lax.div(i * tq + (tq - 1), jnp.int32(kc)) + 1

    for n in range(span, n_kc + 1, span):
        @pl.when((n_c > n - span) & (n_c <= n))
        def _(n=n):
            wide = n * kc
            kb = k_ref[0, 0:wide, :].astype(BF16)
            vb = v_ref[0, 0:wide, :].astype(BF16)
            bias = jnp.concatenate([bias_ref[0, c] for c in range(n)], axis=1)
            qpos = lax.broadcasted_iota(jnp.int32, (tq, wide), 0) + i * tq
            kpos = lax.broadcasted_iota(jnp.int32, (tq, wide), 1)
            s = _dot_nt(qb, kb) * scale - slope * (qpos - kpos).astype(F32) + bias.astype(F32)
            p = jnp.exp(s - jnp.max(s, axis=-1, keepdims=True))
            l = jnp.sum(p, axis=-1, keepdims=True)
            o_ref[0] = _dot(p.astype(BF16), vb) / l


def _dsa_prompt(q3, k3, v3, bias4, tq=256):
    b, t, w = q3.shape
    nh = w // HEAD_DIM
    n_kc, kc = bias4.shape[1], bias4.shape[3]
    kern = functools.partial(_dsa_prompt_kernel, tq=tq, span=1, scale=HEAD_DIM ** -0.5)
    return pl.pallas_call(
        kern,
        grid=(b, nh, t // tq),
        in_specs=[pl.BlockSpec((1, tq, HEAD_DIM), lambda bi, h, i: (bi, i, h)),
                  pl.BlockSpec((1, t, HEAD_DIM), lambda bi, h, i: (bi, 0, h)),
                  pl.BlockSpec((1, t, HEAD_DIM), lambda bi, h, i: (bi, 0, h)),
                  pl.BlockSpec((1, n_kc, tq, kc), lambda bi, h, i: (bi, 0, i, 0))],
        out_specs=pl.BlockSpec((1, tq, HEAD_DIM), lambda bi, h, i: (bi, i, h)),
        out_shape=jax.ShapeDtypeStruct((b, t, w), F32),
        compiler_params=_params(("arbitrary",) * 3),
        name="dsa_prompt",
    )(q3, k3, v3, bias4)


QROWS = 8


def _page_specs(n_per_step, n_pages, page_shape, reverse):
    specs = []
    for p in range(n_per_step):
        def imap(b, g, pt, p=p):
            j = jnp.minimum(g * n_per_step + p, n_pages - 1)
            if reverse:
                j = n_pages - 1 - j
            return (pt[b * n_pages + j],) + (0,) * len(page_shape)
        specs.append(pl.BlockSpec((1,) + tuple(page_shape), imap))
    return specs


def _pool_rows(pool):
    return pool.reshape(pool.shape[0], pool.shape[1] * pool.shape[2], pool.shape[3])


def _paged_heads(ref, n_h):
    return lambda h: ref[0, pl.ds(h, PAGE_SIZE, stride=n_h), :]


def _flat_heads(ref):
    return lambda h: ref[:, h * HEAD_DIM:(h + 1) * HEAD_DIM]


def _heads_qk(q, get_k, n_h):
    return jnp.concatenate(
        [_dot_nt(q[h * QROWS:(h + 1) * QROWS].astype(BF16), get_k(h).astype(BF16))
         for h in range(n_h)], axis=0)


def _heads_pv(p, get_v, n_h):
    return jnp.concatenate(
        [_dot(p[h * QROWS:(h + 1) * QROWS].astype(BF16), get_v(h).astype(BF16))
         for h in range(n_h)], axis=0)


def _store_heads(o_ref, acc, n_new, n_h):
    for h in range(n_h):
        o_ref[0, :, h * HEAD_DIM:(h + 1) * HEAD_DIM] = acc[h * QROWS:h * QROWS + n_new, :]


def _fill_new_rows(pad_ref, new_ref, n_new):
    pad_ref[...] = jnp.zeros_like(pad_ref)
    pad_ref[0:n_new, :] = new_ref[0]


def _sb_sample_kernel(pt_ref, q_ref, knew_ref, vnew_ref, *rest, n_pp, n_pages, n_new, scale):
    k_refs = rest[:n_pp]
    v_refs = rest[n_pp:2 * n_pp]
    o_ref, acc_ref, run_ref, kpad_ref, vpad_ref = rest[2 * n_pp:]
    g = pl.program_id(1)
    past_len = n_pages * PAGE_SIZE
    rows = q_ref.shape[1]
    q = q_ref[0]
    umat = _suffix_matrix(PAGE_SIZE)
    col = lax.broadcasted_iota(jnp.int32, (rows, PAGE_SIZE), 1)
    qi = _mod_pow2(lax.broadcasted_iota(jnp.int32, (rows, PAGE_SIZE), 0), QROWS)
    qpos = past_len + jnp.minimum(qi, n_new - 1)

    def visit(get_ks, get_vs, mask, acc, run):
        zs = [_heads_qk(q, get_k, H_SB) * scale for get_k in get_ks]
        a_blocks, run = _stick_blocks(zs, mask, run, umat)
        for a, get_v in zip(a_blocks, get_vs):
            acc = acc + _heads_pv(a, get_v, H_SB)
        return acc, run

    @pl.when(g == 0)
    def _():
        _fill_new_rows(kpad_ref, knew_ref, n_new)
        _fill_new_rows(vpad_ref, vnew_ref, n_new)
        acc, run = visit([_flat_heads(kpad_ref)], [_flat_heads(vpad_ref)], (col + past_len) < qpos,
                         jnp.zeros(acc_ref.shape, F32), jnp.zeros(run_ref.shape, F32))
        acc_ref[...] = acc
        run_ref[...] = run

    acc, run = visit([_paged_heads(r, H_SB) for r in k_refs], [_paged_heads(r, H_SB) for r in v_refs],
                     None, acc_ref[...], run_ref[...])
    acc_ref[...] = acc
    run_ref[...] = run

    @pl.when(g == pl.num_programs(1) - 1)
    def _():
        _store_heads(o_ref, acc, n_new, H_SB)


def _sb_sample(page_table, q, knew, vnew, pool_k, pool_v, n_pp=8):
    db, n_pages = page_table.shape
    n_pp = min(n_pp, n_pages)
    n_new, w = knew.shape[1], knew.shape[2]
    rows = q.shape[1]
    pool_k, pool_v = _pool_rows(pool_k), _pool_rows(pool_v)
    tail = pool_k.shape[1:]
    kern = functools.partial(_sb_sample_kernel, n_pp=n_pp, n_pages=n_pages, n_new=n_new,
                             scale=HEAD_DIM ** -0.5)
    row_spec = lambda r, c: pl.BlockSpec((1, r, c), lambda b, g, pt: (b, 0, 0))
    grid_spec = pltpu.PrefetchScalarGridSpec(
        num_scalar_prefetch=1,
        grid=(db, n_pages // n_pp),
        in_specs=[row_spec(rows, HEAD_DIM), row_spec(n_new, w), row_spec(n_new, w)]
        + _page_specs(n_pp, n_pages, tail, True) + _page_specs(n_pp, n_pages, tail, True),
        out_specs=row_spec(n_new, w),
        scratch_shapes=[pltpu.VMEM((rows, HEAD_DIM), F32), pltpu.VMEM((rows, 1), F32),
                        pltpu.VMEM((PAGE_SIZE, w), F32), pltpu.VMEM((PAGE_SIZE, w), F32)],
    )
    return pl.pallas_call(
        kern, grid_spec=grid_spec,
        out_shape=jax.ShapeDtypeStruct((db, n_new, w), F32),
        compiler_params=_params(("arbitrary",) * 2),
        name="sb_sample",
    )(page_table.reshape(-1), q, knew, vnew, *([pool_k] * n_pp), *([pool_v] * n_pp))


def _idx_sample_kernel(pt_ref, iq_ref, w_ref, iknew_ref, *rest, n_pp, n_new):
    ik_refs = rest[:n_pp]
    o_ref, pad_ref = rest[n_pp:]
    g = pl.program_id(1)
    iq = iq_ref[0].astype(BF16)
    w = w_ref[0]

    def page_scores(ikpage):
        s = jnp.maximum(_dot_nt(iq, ikpage.astype(BF16)), 0.0) * w
        return jnp.sum(s.reshape(n_new, H_IDX, PAGE_SIZE), axis=1)

    @pl.when(g < pl.num_programs(1) - 1)
    def _():
        for p in range(n_pp):
            o_ref[0, :, p * PAGE_SIZE:(p + 1) * PAGE_SIZE] = page_scores(ik_refs[p][0])

    @pl.when(g == pl.num_programs(1) - 1)
    def _():
        pad_ref[...] = jnp.zeros_like(pad_ref)
        pad_ref[0:n_new, :] = iknew_ref[0]
        o_ref[0] = jnp.zeros(o_ref.shape[1:], F32)
        o_ref[0, :, 0:PAGE_SIZE] = page_scores(pad_ref[...])


def _idx_sample(page_table, iq, wcol, iknew, pool_ik, n_pp=16):
    db, n_pages = page_table.shape
    n_pp = min(n_pp, n_pages)
    n_new = iknew.shape[1]
    rows = iq.shape[1]
    n_groups = n_pages // n_pp
    kern = functools.partial(_idx_sample_kernel, n_pp=n_pp, n_new=n_new)
    grid_spec = pltpu.PrefetchScalarGridSpec(
        num_scalar_prefetch=1,
        grid=(db, n_groups + 1),
        in_specs=[pl.BlockSpec((1, rows, D_IDX), lambda b, g, pt: (b, 0, 0)),
                  pl.BlockSpec((1, rows, 1), lambda b, g, pt: (b, 0, 0)),
                  pl.BlockSpec((1, n_new, D_IDX), lambda b, g, pt: (b, 0, 0))]
        + _page_specs(n_pp, n_pages, (PAGE_SIZE, D_IDX), False),
        out_specs=pl.BlockSpec((1, n_new, n_pp * PAGE_SIZE), lambda b, g, pt: (b, 0, g)),
        scratch_shapes=[pltpu.VMEM((PAGE_SIZE, D_IDX), F32)],
    )
    return pl.pallas_call(
        kern, grid_spec=grid_spec,
        out_shape=jax.ShapeDtypeStruct((db, n_new, (n_groups + 1) * n_pp * PAGE_SIZE), F32),
        compiler_params=_params(("arbitrary",) * 2),
        name="idx_sample",
    )(page_table.reshape(-1), iq, wcol, iknew, *([pool_ik] * n_pp))


def _topk_mask_kernel(s_ref, o_ref, *, n_new, past_len, topk):
    shape = s_ref.shape
    qpos = past_len + _mod_pow2(lax.broadcasted_iota(jnp.int32, shape, 0), n_new)
    kpos = lax.broadcasted_iota(jnp.int32, shape, 1)
    o_ref[...] = _topk_bias(s_ref[...], kpos <= qpos, topk)


def _topk_mask(scores2, n_new, past_len, topk):
    kern = functools.partial(_topk_mask_kernel, n_new=n_new, past_len=past_len, topk=float(topk))
    return pl.pallas_call(
        kern,
        grid=(1,),
        in_specs=[pl.BlockSpec(scores2.shape, lambda i: (0, 0))],
        out_specs=pl.BlockSpec(scores2.shape, lambda i: (0, 0)),
        out_shape=jax.ShapeDtypeStruct(scores2.shape, F32),
        compiler_params=_params(("arbitrary",)),
        name="topk_mask",
    )(scores2)


def _dsa_sample_kernel(pt_ref, q_ref, knew_ref, vnew_ref, bias_ref, bias_new_ref, *rest,
                       n_pp, n_pages, n_new, scale):
    k_refs = rest[:n_pp]
    v_refs = rest[n_pp:2 * n_pp]
    o_ref, acc_ref, m_ref, l_ref, kpad_ref, vpad_ref = rest[2 * n_pp:]
    g = pl.program_id(1)
    past_len = n_pages * PAGE_SIZE
    rows = q_ref.shape[1]
    q = q_ref[0]

    def visit(get_ks, get_vs, bias4, kpos0, acc, m, l):
        width = len(get_ks) * PAGE_SIZE
        rid = lax.broadcasted_iota(jnp.int32, (rows, width), 0)
        kpos = lax.broadcasted_iota(jnp.int32, (rows, width), 1) + kpos0
        qi = _mod_pow2(rid, QROWS)
        slope = jnp.exp2(-(_div_pow2(rid, QROWS) + 1).astype(F32))
        qpos = past_len + jnp.minimum(qi, n_new - 1)
        bias = jnp.zeros((rows, width), F32)
        for r in range(n_new):
            bias = jnp.where(qi == r, bias4[r:r + 1, :], bias)
        s = [_heads_qk(q, get_k, H_DSA) for get_k in get_ks]
        s = (s[0] if len(s) == 1 else jnp.concatenate(s, axis=1)) * scale
        s = s - slope * (qpos - kpos).astype(F32) + bias
        m_new = jnp.maximum(m, jnp.max(s, axis=-1, keepdims=True))
        alpha = jnp.exp(m - m_new)
        p = jnp.exp(s - m_new)
        l = alpha * l + jnp.sum(p, axis=-1, keepdims=True)
        acc = alpha * acc
        for j, get_v in enumerate(get_vs):
            acc = acc + _heads_pv(p[:, j * PAGE_SIZE:(j + 1) * PAGE_SIZE], get_v, H_DSA)
        return acc, m_new, l

    @pl.when(g == 0)
    def _():
        _fill_new_rows(kpad_ref, knew_ref, n_new)
        _fill_new_rows(vpad_ref, vnew_ref, n_new)
        acc, m, l = visit([_flat_heads(kpad_ref)], [_flat_heads(vpad_ref)], bias_new_ref[0],
                          past_len, jnp.zeros(acc_ref.shape, F32),
                          jnp.full(m_ref.shape, NEG_INF, F32), jnp.zeros(l_ref.shape, F32))
        acc_ref[...] = acc
        m_ref[...] = m
        l_ref[...] = l

    acc, m, l = visit([_paged_heads(r, H_DSA) for r in k_refs], [_paged_heads(r, H_DSA) for r in v_refs],
                      bias_ref[0], g * (n_pp * PAGE_SIZE), acc_ref[...], m_ref[...], l_ref[...])
    acc_ref[...] = acc
    m_ref[...] = m
    l_ref[...] = l

    @pl.when(g == pl.num_programs(1) - 1)
    def _():
        _store_heads(o_ref, acc / l, n_new, H_DSA)


def _dsa_sample(page_table, q, knew, vnew, bias3, pool_k, pool_v, n_pp=8):
    db, n_pages = page_table.shape
    n_pp = min(n_pp, n_pages)
    n_new, w = knew.shape[1], knew.shape[2]
    rows = q.shape[1]
    pool_k, pool_v = _pool_rows(pool_k), _pool_rows(pool_v)
    tail = pool_k.shape[1:]
    kern = functools.partial(_dsa_sample_kernel, n_pp=n_pp, n_pages=n_pages, n_new=n_new,
                             scale=HEAD_DIM ** -0.5)
    row_spec = lambda r, c: pl.BlockSpec((1, r, c), lambda b, g, pt: (b, 0, 0))
    grid_spec = pltpu.PrefetchScalarGridSpec(
        num_scalar_prefetch=1,
        grid=(db, n_pages // n_pp),
        in_specs=[row_spec(rows, HEAD_DIM), row_spec(n_new, w), row_spec(n_new, w),
                  pl.BlockSpec((1, n_new, n_pp * PAGE_SIZE), lambda b, g, pt: (b, 0, g)),
                  pl.BlockSpec((1, n_new, PAGE_SIZE), lambda b, g, pt: (b, 0, n_pages))]
        + _page_specs(n_pp, n_pages, tail, False) + _page_specs(n_pp, n_pages, tail, False),
        out_specs=row_spec(n_new, w),
        scratch_shapes=[pltpu.VMEM((rows, HEAD_DIM), F32), pltpu.VMEM((rows, 1), F32),
                        pltpu.VMEM((rows, 1), F32),
                        pltpu.VMEM((PAGE_SIZE, w), F32), pltpu.VMEM((PAGE_SIZE, w), F32)],
    )
    return pl.pallas_call(
        kern, grid_spec=grid_spec,
        out_shape=jax.ShapeDtypeStruct((db, n_new, w), F32),
        compiler_params=_params(("arbitrary",) * 2),
        name="dsa_sample",
    )(page_table.reshape(-1), q, knew, vnew, bias3, bias3, *([pool_k] * n_pp), *([pool_v] * n_pp))


def _merge_kernel(so_ref, sz_ref, do_ref, dz_ref, gs_ref, gd_ref, ws_ref, wd_ref, o_ref):
    sz = sz_ref[...]
    dz = dz_ref[...]
    a_sb = (so_ref[...] * (sz * jax.nn.sigmoid(sz))).astype(BF16)
    a_ds = (do_ref[...] * (dz * jax.nn.sigmoid(dz))).astype(BF16)
    h_sb = _dot(a_sb, ws_ref[...])
    h_ds = _dot(a_ds, wd_ref[...])
    o_ref[...] = (jax.nn.sigmoid(gs_ref[...]) * h_sb
                  + jax.nn.sigmoid(gd_ref[...]) * h_ds).astype(o_ref.dtype)


def _merge(sb_o, sb_z, ds_o, ds_z, g_sb, g_ds, w_up_sb, w_up_dsa, tm):
    m, w = sb_o.shape
    d = g_sb.shape[1]
    act = pl.BlockSpec((tm, w), lambda i: (i, 0))
    gate = pl.BlockSpec((tm, d), lambda i: (i, 0))
    wgt = pl.BlockSpec((w, d), lambda i: (0, 0))
    return pl.pallas_call(
        _merge_kernel,
        grid=(m // tm,),
        in_specs=[act, act, act, act, gate, gate, wgt, wgt],
        out_specs=gate,
        out_shape=jax.ShapeDtypeStruct((m, d), BF16),
        compiler_params=_params(("arbitrary",)),
        name="merge",
    )(sb_o, sb_z, ds_o, ds_z, g_sb, g_ds, w_up_sb, w_up_dsa)


def _out_kernel(mg_ref, w_ref, x_ref, gate_ref, fg_ref, o_ref):
    y = x_ref[0] + gate_ref[0] * _dot(mg_ref[0], w_ref[...])
    o_ref[0] = y * lax.rsqrt(jnp.mean(y * y, axis=-1, keepdims=True) + RMS_EPS) * fg_ref[...]


def _out(merged3, w_out, x3, gate3, final_g, tr):
    gn, r, d = x3.shape
    mr = gate3.shape[1]
    mblk = 1 if mr == 1 else tr
    gate_map = (lambda b, i: (b, 0, 0)) if mr == 1 else (lambda b, i: (b, i, 0))
    tile = pl.BlockSpec((1, tr, d), lambda b, i: (b, i, 0))
    return pl.pallas_call(
        _out_kernel,
        grid=(gn, r // tr),
        in_specs=[tile, pl.BlockSpec((d, d), lambda b, i: (0, 0)), tile,
                  pl.BlockSpec((1, mblk, d), gate_map),
                  pl.BlockSpec((1, d), lambda b, i: (0, 0))],
        out_specs=tile,
        out_shape=jax.ShapeDtypeStruct((gn, r, d), F32),
        compiler_params=_params(("arbitrary",) * 2),
        name="out_proj",
    )(merged3, w_out, x3, gate3, final_g.reshape(1, d))


def _split_w_in(w_in, d_model):
    w_sb, w_ds = H_SB * HEAD_DIM, H_DSA * HEAD_DIM
    sizes = (("sb_q", w_sb), ("sb_k", w_sb), ("sb_v", w_sb), ("sb_z", w_sb),
             ("ds_q", w_ds), ("ds_k", w_ds), ("ds_v", w_ds), ("ds_z", w_ds),
             ("ix_q", H_IDX * D_IDX), ("ix_k", D_IDX), ("ix_w", H_IDX),
             ("g_sb", d_model), ("g_ds", d_model))
    pieces, off = {}, 0
    for name, n in sizes:
        piece = w_in[:, off:off + n].astype(BF16)
        if n % LANES:
            piece = jnp.pad(piece, ((0, 0), (0, LANES - n % LANES)))
        pieces[name] = piece
        off += n
    assert off == w_in.shape[1]
    return pieces


def _project(xn2, w_pieces, tm):
    return {name: _mm(xn2, w, tm, "proj_" + name) for name, w in w_pieces.items()}


def _head_major_queries(q, n_h):
    b, tq, _ = q.shape
    qh = q.reshape(b, tq, n_h, HEAD_DIM).transpose(0, 2, 1, 3)
    return jnp.pad(qh, ((0, 0), (0, 0), (0, QROWS - tq), (0, 0))).reshape(b, n_h * QROWS, HEAD_DIM)


def kernel(x_prompt, x_sample, c_prompt, c_sample, cache_sb_k, cache_sb_v, cache_dsa_k,
           cache_dsa_v, cache_idx_k, page_table, norm_g, w_ada, b_ada, w_in, w_up_sb,
           w_up_dsa, w_out, final_g):
    bp, t, d = x_prompt.shape
    db, n_new, _ = x_sample.shape
    depth = norm_g.shape[0]
    assert depth == 1, "the final rmsnorm is fused into the layer's output projection"
    n_pool = cache_sb_k.shape[1]
    n_pages = page_table.shape[1]
    past_len = n_pages * PAGE_SIZE
    topk_p = min(TOPK_MAX, t // 4)
    topk_s = min(TOPK_MAX, (past_len + n_new) // 4)
    w_sb, w_ds = H_SB * HEAD_DIM, H_DSA * HEAD_DIM

    xp, xs = x_prompt, x_sample
    n_c = bp + db
    c_all = jnp.concatenate([c_prompt, c_sample, jnp.zeros((-n_c % 16, d), F32)], axis=0)
    rows = [[] for _ in range(10)]
    for l in range(depth):
        mod = _ada_mod(c_all, w_ada[l], b_ada[l])
        mod_p = mod[:bp].reshape(bp, 1, 3 * d)
        mod_s = jnp.repeat(mod[bp:n_c], n_new, axis=0).reshape(1, db * n_new, 3 * d)
        w_pieces = _split_w_in(w_in[l], d)
        wu_sb, wu_ds, wo = w_up_sb[l].astype(BF16), w_up_dsa[l].astype(BF16), w_out[l].astype(BF16)

        xn_p = _xn(xp, norm_g[l], mod_p[:, :, d:2 * d], mod_p[:, :, :d], tr=512)
        p = _project(xn_p.reshape(bp * t, d), w_pieces, tm=1024)
        r3 = lambda a: a.reshape(bp, t, a.shape[-1])
        sb_o = _sb_prompt(r3(p["sb_q"]), r3(p["sb_k"]), r3(p["sb_v"]))
        bias = _idx_prompt(r3(p["ix_q"]), r3(p["ix_w"]), r3(p["ix_k"]), topk_p)
        ds_o = _dsa_prompt(r3(p["ds_q"]), r3(p["ds_k"]), r3(p["ds_v"]), bias)
        merged = _merge(sb_o.reshape(bp * t, w_sb), p["sb_z"], ds_o.reshape(bp * t, w_ds), p["ds_z"],
                        p["g_sb"], p["g_ds"], wu_sb, wu_ds, tm=256)
        xp_new = _out(merged.reshape(bp, t, d), wo, xp, mod_p[:, :, 2 * d:], final_g, tr=256)

        m_s = db * n_new
        xn_s = _xn(xs.reshape(1, m_s, d), norm_g[l], mod_s[:, :, d:2 * d], mod_s[:, :, :d], tr=m_s)
        s = _project(xn_s.reshape(m_s, d), w_pieces, tm=m_s)
        s3 = lambda a: a.reshape(db, n_new, a.shape[-1])
        sb_os = _sb_sample(page_table, _head_major_queries(s3(s["sb_q"]), H_SB),
                           s3(s["sb_k"]), s3(s["sb_v"]), cache_sb_k[l], cache_sb_v[l])
        wcol = (s3(s["ix_w"])[:, :, :H_IDX] * ((D_IDX ** -0.5) * (H_IDX ** -0.5))
                ).reshape(db, n_new * H_IDX, 1)
        scores = _idx_sample(page_table, s["ix_q"].reshape(db, n_new * H_IDX, D_IDX), wcol,
                             s3(s["ix_k"]), cache_idx_k[l])
        n_sc = scores.shape[-1]
        bias_s = _topk_mask(scores.reshape(m_s, n_sc), n_new, past_len, topk_s)
        ds_os = _dsa_sample(page_table, _head_major_queries(s3(s["ds_q"]), H_DSA),
                            s3(s["ds_k"]), s3(s["ds_v"]), bias_s.reshape(db, n_new, n_sc),
                            cache_dsa_k[l], cache_dsa_v[l])
        merged_s = _merge(sb_os.reshape(m_s, w_sb), s["sb_z"], ds_os.reshape(m_s, w_ds), s["ds_z"],
                          s["g_sb"], s["g_ds"], wu_sb, wu_ds, tm=m_s)
        xs_new = _out(merged_s.reshape(1, m_s, d), wo, xs.reshape(1, m_s, d), mod_s[:, :, 2 * d:],
                      final_g, tr=m_s)

        for lst, r in zip(rows, (p["sb_k"].reshape(bp, t, H_SB, HEAD_DIM),
                                 p["sb_v"].reshape(bp, t, H_SB, HEAD_DIM),
                                 p["ds_k"].reshape(bp, t, H_DSA, HEAD_DIM),
                                 p["ds_v"].reshape(bp, t, H_DSA, HEAD_DIM),
                                 p["ix_k"].reshape(bp, t, D_IDX),
                                 s["sb_k"].reshape(db, n_new, H_SB, HEAD_DIM),
                                 s["sb_v"].reshape(db, n_new, H_SB, HEAD_DIM),
                                 s["ds_k"].reshape(db, n_new, H_DSA, HEAD_DIM),
                                 s["ds_v"].reshape(db, n_new, H_DSA, HEAD_DIM),
                                 s["ix_k"].reshape(db, n_new, D_IDX))):
            lst.append(r)
        xp, xs = xp_new, xs_new.reshape(db, n_new, d)

    caches = [r[0][None] if depth == 1 else jnp.stack(r, axis=0) for r in rows]
    return (xp, xs, *caches)
```
